```python
import math
import jax, jax.numpy as jnp
from jax import lax
import numpy as np

D_MODEL = 1024
BATCH = 8
SEQ = 4096
DEPTH = 2

GRID_W = 64
CTX_LEN = 256
EPS = 1e-6
ROPE_THETA = 10000.0
Q_BLOCK = 128
SCAN_CHUNK = 128
FFN_HIDDEN = int(math.ceil(8 * D_MODEL / 3 / 256)) * 256

HEAD_DIM = 64
GQA_HQ = D_MODEL // (2 * HEAD_DIM)
GQA_HKV = GQA_HQ // 4
GQA_REP = GQA_HQ // GQA_HKV
GQA_DH = HEAD_DIM
MLA_H = D_MODEL // (2 * HEAD_DIM)
MLA_DNOPE = HEAD_DIM
MLA_DROPE = HEAD_DIM // 2
MLA_DQK = MLA_DNOPE + MLA_DROPE
MLA_DV = HEAD_DIM
MLA_Q_RANK = 3 * D_MODEL // 8
MLA_KV_RANK = D_MODEL // 4
ATTN_SPLITS = (GQA_HQ * GQA_DH, GQA_HKV * GQA_DH, GQA_HKV * GQA_DH, MLA_Q_RANK, MLA_KV_RANK, MLA_DROPE)
ATTN_IN = sum(ATTN_SPLITS)
ATTN_OUT = GQA_HQ * GQA_DH + MLA_H * MLA_DV
SSD_DINNER = D_MODEL
SSD_P = 64
SSD_H = SSD_DINNER // SSD_P
SSD_G = 2
SSD_HG = SSD_H // SSD_G
SSD_N = 128
SSD_CONV = 5
SSD_CONV_DIM = SSD_DINNER + 2 * SSD_G * SSD_N
RET_H = 8
RET_DK = D_MODEL // (2 * RET_H)
RET_DV = 2 * RET_DK
SSM_SPLITS = (SSD_DINNER, SSD_CONV_DIM, 2 * SSD_H, RET_H * RET_DK, RET_H * RET_DK, RET_H * RET_DV, RET_H * RET_DV)
SSM_IN = sum(SSM_SPLITS)
SSM_OUT = SSD_DINNER + RET_H * RET_DV

F32 = jnp.float32

kernel_name = 'hybrid_diffusion_prefix_backbone'


def split_cols(h, sizes):
    idx = np.cumsum(sizes)[:-1].tolist()
    return jnp.split(h, idx, axis=-1)


def rms_norm(x, w):
    xf = x.astype(F32)
    y = xf * lax.rsqrt(jnp.mean(xf * xf, axis=-1, keepdims=True) + EPS)
    return (y * w).astype(x.dtype)


def head_layer_norm(x, w):
    xf = x.astype(F32)
    mu = jnp.mean(xf, axis=-1, keepdims=True)
    var = jnp.mean(jnp.square(xf - mu), axis=-1, keepdims=True)
    return ((xf - mu) * lax.rsqrt(var + EPS) * w).astype(x.dtype)


def adaln(x, w, shift, scale):
    return rms_norm(x, w) * (1.0 + scale) + shift


def axial_rope_tables(seq_len, dim):
    rows = seq_len // GRID_W
    rr, cc = jnp.meshgrid(jnp.arange(rows, dtype=F32), jnp.arange(GRID_W, dtype=F32), indexing='ij')
    quarter = dim // 4
    inv = ROPE_THETA ** (-jnp.arange(quarter, dtype=F32) / quarter)
    ang = jnp.concatenate([rr.reshape(-1)[:, None] * inv, cc.reshape(-1)[:, None] * inv], axis=-1)
    cos = jnp.concatenate([jnp.ones((CTX_LEN, dim // 2), F32), jnp.cos(ang)], axis=0)
    sin = jnp.concatenate([jnp.zeros((CTX_LEN, dim // 2), F32), jnp.sin(ang)], axis=0)
    return cos, sin


def apply_rope(x, cos, sin):
    xf = x.astype(F32).reshape(x.shape[:-1] + (x.shape[-1] // 2, 2))
    x0, x1 = xf[..., 0], xf[..., 1]
    c = cos[None, :, None, :]
    s = sin[None, :, None, :]
    out = jnp.stack([x0 * c - x1 * s, x0 * s + x1 * c], axis=-1)
    return out.reshape(x.shape).astype(x.dtype)


def block_attention(q, k, v, scale):
    b, s, hk, r, dq = q.shape
    nb = s // Q_BLOCK
    qb = jnp.moveaxis(q.reshape(b, nb, Q_BLOCK, hk, r, dq), 1, 0)

    def attend(qi):
        sc = jnp.einsum('bqgrd,blgd->bgrql', qi, k).astype(F32) * scale
        p = jax.nn.softmax(sc, axis=-1).astype(v.dtype)
        return jnp.einsum('bgrql,blge->bqgre', p, v)

    o = lax.map(attend, qb)
    return jnp.moveaxis(o, 0, 1).reshape(b, s, hk, r, v.shape[-1])


def two_way_attention(q, k, v, scale, need_ctx):
    o_lat = block_attention(q[:, CTX_LEN:], k, v, scale)
    if not need_ctx:
        return o_lat
    o_ctx = block_attention(q[:, :CTX_LEN], k[:, :CTX_LEN], v[:, :CTX_LEN], scale)
    return jnp.concatenate([o_ctx, o_lat], axis=1)


def chunked_scan(q, k, v, log_a, state0):
    b, l, g, n = q.shape
    hg, p = v.shape[3], v.shape[4]
    nc = l // SCAN_CHUNK
    q = q.reshape(b, nc, SCAN_CHUNK, g, n)
    k = k.reshape(b, nc, SCAN_CHUNK, g, n)
    v = v.reshape(b, nc, SCAN_CHUNK, g, hg, p)
    a_cum = jnp.cumsum(log_a.astype(F32).reshape(b, nc, SCAN_CHUNK, g, hg), axis=2)
    lower = jnp.tril(jnp.ones((SCAN_CHUNK, SCAN_CHUNK), dtype=bool))[:, :, None, None]
    seg = a_cum[:, :, :, None] - a_cum[:, :, None, :]
    decay = jnp.exp(jnp.where(lower, seg, -jnp.inf))
    scores = jnp.einsum('bcign,bcjgn->bcijg', q, k).astype(F32)
    y_diag = jnp.einsum('bcijgh,bcjghp->bcighp', scores[..., None] * decay, v)
    decay_end = jnp.exp(a_cum[:, :, -1:] - a_cum)
    chunk_states = jnp.einsum('bcjgn,bcjgh,bcjghp->bcghpn', k, decay_end, v)
    chunk_decay = jnp.exp(a_cum[:, :, -1])

    def step(s, inp):
        st, dc = inp
        return s * dc[..., None, None] + st, s

    final, s_in = lax.scan(step, state0, (jnp.moveaxis(chunk_states, 1, 0), jnp.moveaxis(chunk_decay, 1, 0)))
    s_in = jnp.moveaxis(s_in, 0, 1)
    y_off = jnp.einsum('bcign,bcghpn,bcigh->bcighp', q, s_in, jnp.exp(a_cum))
    return (y_diag + y_off).reshape(b, l, g, hg, p), final


def bidir_scan(q, k, v_f, v_b, la_f, la_b):
    g, hg, p = v_f.shape[2], v_f.shape[3], v_f.shape[4]
    s0 = jnp.zeros((q.shape[0], g, hg, p, q.shape[-1]), F32)
    cl = CTX_LEN

    def rev(a):
        return jnp.flip(a, axis=1)

    yc_f, sc_f = chunked_scan(q[:, :cl], k[:, :cl], v_f[:, :cl], la_f[:, :cl], s0)
    yl_f, _ = chunked_scan(q[:, cl:], k[:, cl:], v_f[:, cl:], la_f[:, cl:], sc_f)
    yc_b, sc_b = chunked_scan(rev(q[:, :cl]), rev(k[:, :cl]), rev(v_b[:, :cl]), rev(la_b[:, :cl]), s0)
    yl_b, _ = chunked_scan(rev(q[:, cl:]), rev(k[:, cl:]), rev(v_b[:, cl:]), rev(la_b[:, cl:]), sc_b)
    return jnp.concatenate([yc_f + rev(yc_b), yl_f + rev(yl_b)], axis=1)


def dwconv(x, w, bias):
    kw = w.shape[0]
    y = lax.conv_general_dilated(x, w[:, None, :], window_strides=(1,), padding=[(kw // 2, kw // 2)],
                                 dimension_numbers=('NWC', 'WIO', 'NWC'), feature_group_count=x.shape[-1])
    return y + bias


def swiglu(h, w13, w2):
    gte, up = jnp.split(h @ w13, 2, axis=-1)
    return (jax.nn.silu(gte) * up) @ w2


def attn_mixer(u, w_in, gqa_qn, gqa_kn, mla_qa_norm, mla_wq_b, mla_kva_norm, mla_wkv_b, mla_qn, mla_kn, w_out,
               rope_gqa, rope_mla, need_ctx):
    b, n, _ = u.shape
    qa, ka, va, q_lora, kv_lora, k_rope = split_cols(u @ w_in, ATTN_SPLITS)
    qa = apply_rope(rms_norm(qa.reshape(b, n, GQA_HQ, GQA_DH), gqa_qn), *rope_gqa)
    ka = apply_rope(rms_norm(ka.reshape(b, n, GQA_HKV, GQA_DH), gqa_kn), *rope_gqa)
    va = va.reshape(b, n, GQA_HKV, GQA_DH)
    o_a = two_way_attention(qa.reshape(b, n, GQA_HKV, GQA_REP, GQA_DH), ka, va, GQA_DH ** -0.5, need_ctx)
    o_a = o_a.reshape(o_a.shape[0], o_a.shape[1], GQA_HQ * GQA_DH)
    qm = (rms_norm(q_lora, mla_qa_norm) @ mla_wq_b).reshape(b, n, MLA_H, MLA_DQK)
    kv = (rms_norm(kv_lora, mla_kva_norm) @ mla_wkv_b).reshape(b, n, MLA_H, MLA_DNOPE + MLA_DV)
    k_nope, vm = kv[..., :MLA_DNOPE], kv[..., MLA_DNOPE:]
    km = jnp.concatenate([k_nope, jnp.broadcast_to(k_rope[:, :, None, :], (b, n, MLA_H, MLA_DROPE))], axis=-1)
    qm = rms_norm(qm, mla_qn)
    km = rms_norm(km, mla_kn)
    qm = jnp.concatenate([qm[..., :MLA_DNOPE], apply_rope(qm[..., MLA_DNOPE:], *rope_mla)], axis=-1)
    km = jnp.concatenate([km[..., :MLA_DNOPE], apply_rope(km[..., MLA_DNOPE:], *rope_mla)], axis=-1)
    o_b = two_way_attention(qm[:, :, :, None, :], km, vm, MLA_DQK ** -0.5, need_ctx)
    o_b = o_b.reshape(o_b.shape[0], o_b.shape[1], MLA_H * MLA_DV)
    return jnp.concatenate([o_a, o_b], axis=-1) @ w_out


def ssm_mixer(u, w_in, conv_w, conv_b, dt_bias, a_log, d_skip, ssd_norm, ret_logit, ret_norm, w_out,
              rope_ret, need_ctx):
    b, n, _ = u.shape
    z, xbc, dt, rq, rk, rv, rg = split_cols(u @ w_in, SSM_SPLITS)
    xbc = jax.nn.silu(jnp.concatenate([dwconv(xbc[:, :CTX_LEN], conv_w, conv_b),
                                       dwconv(xbc[:, CTX_LEN:], conv_w, conv_b)], axis=1))
    xs, bm, cm = split_cols(xbc, (SSD_DINNER, SSD_G * SSD_N, SSD_G * SSD_N))
    xs = xs.reshape(b, n, SSD_G, SSD_HG, SSD_P)
    bm = bm.reshape(b, n, SSD_G, SSD_N)
    cm = cm.reshape(b, n, SSD_G, SSD_N)
    dt = jax.nn.softplus(dt.reshape(b, n, 2, SSD_H).astype(F32) + dt_bias).reshape(b, n, 2, SSD_G, SSD_HG)
    a = -jnp.exp(a_log.astype(F32)).reshape(2, SSD_G, SSD_HG)
    la = dt * a
    v_dir = xs[:, :, None] * dt[..., None]
    y = bidir_scan(cm, bm, v_dir[:, :, 0], v_dir[:, :, 1], la[:, :, 0], la[:, :, 1])
    y = y + d_skip.reshape(SSD_G, SSD_HG, 1) * xs
    y = y.reshape(b, n, SSD_G, SSD_HG * SSD_P) * jax.nn.silu(z.reshape(b, n, SSD_G, SSD_HG * SSD_P))
    y = rms_norm(y, ssd_norm.reshape(SSD_G, -1)).reshape(b, n, SSD_DINNER)
    rq = apply_rope(rq.reshape(b, n, RET_H, RET_DK), *rope_ret)
    rk = apply_rope(rk.reshape(b, n, RET_H, RET_DK), *rope_ret) * (RET_DK ** -0.5)
    rv = rv.reshape(b, n, RET_H, 1, RET_DV)
    la_r = jax.nn.log_sigmoid(ret_logit.astype(F32))
    la_f = jnp.broadcast_to(la_r[0][:, None], (b, n, RET_H, 1))
    la_b = jnp.broadcast_to(la_r[1][:, None], (b, n, RET_H, 1))
    yr = bidir_scan(rq, rk, rv, rv, la_f, la_b).reshape(b, n, RET_H, RET_DV)
    yr = head_layer_norm(yr, ret_norm.reshape(RET_H, RET_DV)).reshape(b, n, RET_H * RET_DV) * jax.nn.silu(rg)
    o = jnp.concatenate([y, yr.astype(y.dtype)], axis=-1)
    if not need_ctx:
        o = o[:, CTX_LEN:]
    return o @ w_out


def setup_inputs(seed: int = 0) -> dict:
    key = jax.random.key(seed)
    ks = iter(jax.random.split(key, 48))
    D = D_MODEL
    na = (DEPTH + 1) // 2
    ns = DEPTH // 2

    def nrm(shape, scale):
        return jax.random.normal(next(ks), shape, F32) * scale

    def gain(shape):
        return 1.0 + nrm(shape, 0.02)

    x = nrm((BATCH, SEQ, D), 1.0)
    c = nrm((BATCH, D), 1.0)
    ctx = nrm((BATCH, CTX_LEN, D), 1.0)
    c_ctx = nrm((D,), 1.0)
    mod_w = nrm((DEPTH, D, 6 * D), 0.5 * D ** -0.5)
    mod_b = nrm((DEPTH, 6 * D), 0.01)
    norm1_w = gain((DEPTH, D))
    norm2_w = gain((DEPTH, D))
    ffn_w13 = nrm((DEPTH, D, 2 * FFN_HIDDEN), D ** -0.5)
    ffn_w2 = nrm((DEPTH, FFN_HIDDEN, D), FFN_HIDDEN ** -0.5)
    attn_w_in = nrm((na, D, ATTN_IN), D ** -0.5)
    gqa_qn = gain((na, GQA_DH))
    gqa_kn = gain((na, GQA_DH))
    mla_qa_norm = gain((na, MLA_Q_RANK))
    mla_wq_b = nrm((na, MLA_Q_RANK, MLA_H * MLA_DQK), MLA_Q_RANK ** -0.5)
    mla_kva_norm = gain((na, MLA_KV_RANK))
    mla_wkv_b = nrm((na, MLA_KV_RANK, MLA_H * (MLA_DNOPE + MLA_DV)), MLA_KV_RANK ** -0.5)
    mla_qn = gain((na, MLA_DQK))
    mla_kn = gain((na, MLA_DQK))
    attn_w_out = nrm((na, ATTN_OUT, D), ATTN_OUT ** -0.5)
    ssm_w_in = nrm((ns, D, SSM_IN), D ** -0.5)
    ssd_conv_w = nrm((ns, SSD_CONV, SSD_CONV_DIM), SSD_CONV ** -0.5)
    ssd_conv_b = nrm((ns, SSD_CONV_DIM), 0.01)
    dt0 = jnp.exp(jax.random.uniform(next(ks), (ns, 2, SSD_H), F32, math.log(1e-3), math.log(1e-1)))
    ssd_dt_bias = dt0 + jnp.log(-jnp.expm1(-dt0))
    ssd_a_log = jnp.log(jax.random.uniform(next(ks), (ns, 2, SSD_H), F32, 1.0, 16.0))
    ssd_d = 1.0 + nrm((ns, SSD_H), 0.1)
    ssd_norm = gain((ns, SSD_DINNER))
    hidx = jnp.arange(RET_H, dtype=F32)
    base_logit = jnp.log1p(-jnp.power(2.0, -5.0 - hidx)) + (5.0 + hidx) * math.log(2.0)
    ret_decay_logit = base_logit + nrm((ns, 2, RET_H), 0.01)
    ret_norm = gain((ns, RET_H * RET_DV))
    ssm_w_out = nrm((ns, SSM_OUT, D), SSM_OUT ** -0.5)
    return {'x': x, 'c': c, 'ctx': ctx, 'c_ctx': c_ctx, 'mod_w': mod_w, 'mod_b': mod_b,
            'norm1_w': norm1_w, 'norm2_w': norm2_w, 'ffn_w13': ffn_w13, 'ffn_w2': ffn_w2,
            'attn_w_in': attn_w_in, 'gqa_qn': gqa_qn, 'gqa_kn': gqa_kn, 'mla_qa_norm': mla_qa_norm,
            'mla_wq_b': mla_wq_b, 'mla_kva_norm': mla_kva_norm, 'mla_wkv_b': mla_wkv_b, 'mla_qn': mla_qn,
            'mla_kn': mla_kn, 'attn_w_out': attn_w_out, 'ssm_w_in': ssm_w_in, 'ssd_conv_w': ssd_conv_w,
            'ssd_conv_b': ssd_conv_b, 'ssd_dt_bias': ssd_dt_bias, 'ssd_a_log': ssd_a_log, 'ssd_d': ssd_d,
            'ssd_norm': ssd_norm, 'ret_decay_logit': ret_decay_logit, 'ret_norm': ret_norm, 'ssm_w_out': ssm_w_out}


def reference(x, c, ctx, c_ctx, mod_w, mod_b, norm1_w, norm2_w, ffn_w13, ffn_w2,
              attn_w_in, gqa_qn, gqa_kn, mla_qa_norm, mla_wq_b, mla_kva_norm, mla_wkv_b, mla_qn, mla_kn, attn_w_out,
              ssm_w_in, ssd_conv_w, ssd_conv_b, ssd_dt_bias, ssd_a_log, ssd_d, ssd_norm, ret_decay_logit, ret_norm,
              ssm_w_out):
    seq_len = x.shape[1]
    rope_gqa = axial_rope_tables(seq_len, GQA_DH)
    rope_mla = axial_rope_tables(seq_len, MLA_DROPE)
    rope_ret = axial_rope_tables(seq_len, RET_DK)
    x_ctx, x_lat = ctx, x
    for i in range(DEPTH):
        need_ctx = i < DEPTH - 1
        j = i // 2
        m_lat = jax.nn.silu(c) @ mod_w[i] + mod_b[i]
        m_ctx = jax.nn.silu(c_ctx) @ mod_w[i] + mod_b[i]
        sh1, sc1, g1, sh2, sc2, g2 = [t[:, None] for t in jnp.split(m_lat, 6, axis=-1)]
        ch1, cs1, cg1, ch2, cs2, cg2 = jnp.split(m_ctx, 6, axis=-1)
        u = jnp.concatenate([adaln(x_ctx, norm1_w[i], ch1, cs1), adaln(x_lat, norm1_w[i], sh1, sc1)], axis=1)
        if i % 2 == 0:
            o = attn_mixer(u, attn_w_in[j], gqa_qn[j], gqa_kn[j], mla_qa_norm[j], mla_wq_b[j], mla_kva_norm[j],
                           mla_wkv_b[j], mla_qn[j], mla_kn[j], attn_w_out[j], rope_gqa, rope_mla, need_ctx)
        else:
            o = ssm_mixer(u, ssm_w_in[j], ssd_conv_w[j], ssd_conv_b[j], ssd_dt_bias[j], ssd_a_log[j], ssd_d[j],
                          ssd_norm[j], ret_decay_logit[j], ret_norm[j], ssm_w_out[j], rope_ret, need_ctx)
        x_lat = x_lat + g1 * o[:, o.shape[1] - seq_len:]
        f_lat_in = adaln(x_lat, norm2_w[i], sh2, sc2)
        if need_ctx:
            x_ctx = x_ctx + cg1 * o[:, :CTX_LEN]
            f = swiglu(jnp.concatenate([adaln(x_ctx, norm2_w[i], ch2, cs2), f_lat_in], axis=1), ffn_w13[i], ffn_w2[i])
            x_ctx = x_ctx + cg2 * f[:, :CTX_LEN]
            x_lat = x_lat + g2 * f[:, CTX_LEN:]
        else:
            x_lat = x_lat + g2 * swiglu(f_lat_in, ffn_w13[i], ffn_w2[i])
    return x_lat
```

```python
import functools
import math

import jax
import jax.numpy as jnp
import numpy as np
from jax import lax
from jax.experimental import pallas as pl
from jax.experimental.pallas import tpu as pltpu

F32 = jnp.float32
BF16 = jnp.bfloat16

EPS = 1e-6
ROPE_THETA = 10000.0
GRID_W = 64
HEAD_DIM = 64
LANES = 128
ROW_TILE = 256
SCAN_CHUNK = 128
VMEM_LIMIT = 56 * 1024 * 1024

GQA_HQ = 8
GQA_HKV = 2
MLA_H = 8
MLA_DNOPE = 64
MLA_DROPE = 32
MLA_DQK = MLA_DNOPE + MLA_DROPE
MLA_Q_RANK = 384
MLA_KV_RANK = 256
LOG2E = 1.4426950408889634


def _cparams(sem):
    return pltpu.CompilerParams(dimension_semantics=sem, vmem_limit_bytes=VMEM_LIMIT)


def _const_spec(shape):
    nd = len(shape)
    return pl.BlockSpec(shape, lambda *_: (0,) * nd, pipeline_mode=pl.Buffered(1))


def _rms(x):
    return x * lax.rsqrt(jnp.mean(x * x, axis=-1, keepdims=True) + EPS)


def _swap_pairs(y):
    lane = lax.broadcasted_iota(jnp.int32, y.shape, y.ndim - 1)
    nxt = pltpu.roll(y, LANES - 1, y.ndim - 1)
    prv = pltpu.roll(y, 1, y.ndim - 1)
    return jnp.where(lane % 2 == 0, nxt, prv)


def _rope(y, cos, sin_signed):
    return y * cos + _swap_pairs(y) * sin_signed


def _mod_kernel(c_ref, w_ref, b_ref, o_ref):
    c = c_ref[...]
    a = (c * jax.nn.sigmoid(c)).astype(BF16)
    o_ref[0] = jnp.dot(a, w_ref[0].astype(BF16), preferred_element_type=F32) + b_ref[0]


def _modulation(cvec, mod_w, mod_b):
    depth, d, n = mod_w.shape
    rows = cvec.shape[0]
    tn = 512
    return pl.pallas_call(
        _mod_kernel,
        grid=(depth, n // tn),
        in_specs=[pl.BlockSpec((rows, d), lambda i, j: (0, 0)),
                  pl.BlockSpec((1, d, tn), lambda i, j: (i, 0, j)),
                  pl.BlockSpec((1, 1, tn), lambda i, j: (i, 0, j))],
        out_specs=pl.BlockSpec((1, rows, tn), lambda i, j: (i, 0, j)),
        out_shape=jax.ShapeDtypeStruct((depth, rows, n), F32),
        compiler_params=_cparams(("arbitrary", "arbitrary")),
    )(cvec, mod_w, mod_b.reshape(depth, 1, n))


def _pre_attn_kernel(x_ref, mod_ref, n1_ref, win_ref, gq_ref, gk_ref, gqa_ref, gkva_ref, wqb_ref, wk_ref, wv_ref,
                     gmq_ref, gmk_ref, cg_ref, sg_ref, cm_ref, sm_ref,
                     qg_ref, kg_ref, vg_ref, qm_ref, km_ref, vm_ref):
    x = x_ref[0]
    shift = mod_ref[0, 0, 0:1, :]
    scale = mod_ref[0, 0, 1:2, :]
    u = _rms(x) * n1_ref[...] * (1.0 + scale) + shift
    h = jnp.dot(u.astype(BF16), win_ref[...], preferred_element_type=F32)

    tm = x.shape[0]
    lane = lax.broadcasted_iota(jnp.int32, (tm, LANES), 1)
    lo = lane < HEAD_DIM
    cg, sg, cm, sm = cg_ref[...], sg_ref[...], cm_ref[...], sm_ref[...]

    def pair_norm_rope(blk, gain):
        sq = blk * blk
        ss_lo = jnp.sum(jnp.where(lo, sq, 0.0), axis=-1, keepdims=True)
        ss_hi = jnp.sum(jnp.where(lo, 0.0, sq), axis=-1, keepdims=True)
        r = lax.rsqrt(jnp.where(lo, ss_lo, ss_hi) * (1.0 / HEAD_DIM) + EPS)
        return _rope(blk * r * gain, cg, sg)

    def head_norm_rope(blk, gain):
        r = lax.rsqrt(jnp.sum(blk * blk, axis=-1, keepdims=True) * (1.0 / MLA_DQK) + EPS)
        return _rope(blk * r * gain, cm, sm)

    qscale = (HEAD_DIM ** -0.5) * LOG2E
    for j in range(4):
        y = pair_norm_rope(h[:, LANES * j:LANES * (j + 1)], gq_ref[...]) * qscale
        qg_ref[0, :, LANES * (2 * j):LANES * (2 * j + 1)] = jnp.where(lo, y, 0.0).astype(BF16)
        qg_ref[0, :, LANES * (2 * j + 1):LANES * (2 * j + 2)] = jnp.where(lo, 0.0, y).astype(BF16)
    kg_ref[0] = pair_norm_rope(h[:, 512:640], gk_ref[...]).astype(BF16)
    vg_ref[0] = h[:, 640:768].astype(BF16)

    ql = (_rms(h[:, 768:1152]) * gqa_ref[...]).astype(BF16)
    qm = jnp.dot(ql, wqb_ref[...], preferred_element_type=F32)
    kvl = (_rms(h[:, 1152:1408]) * gkva_ref[...]).astype(BF16)
    kn = jnp.dot(kvl, wk_ref[...], preferred_element_type=F32)
    vm_ref[0] = jnp.dot(kvl, wv_ref[...], preferred_element_type=F32).astype(BF16)
    kr = h[:, 1408:1536]
    mscale = (MLA_DQK ** -0.5) * LOG2E
    for hh in range(MLA_H):
        sl = slice(LANES * hh, LANES * (hh + 1))
        qm_ref[0, :, sl] = (head_norm_rope(qm[:, sl], gmq_ref[...]) * mscale).astype(BF16)
        km_ref[0, :, sl] = head_norm_rope(kn[:, sl] + kr, gmk_ref[...]).astype(BF16)


def _pre_attn(xj, mods, n1, win, gq, gk, gqa, gkva, wqb, wk, wv, gmq, gmk, cg, sg, cm, sm):
    b, n, d = xj.shape
    nt = n // ROW_TILE
    row = lambda w: pl.BlockSpec((1, ROW_TILE, w), lambda i, t: (i, t, 0))
    tab = pl.BlockSpec((ROW_TILE, LANES), lambda i, t: (t, 0))
    outs = [(8 * LANES, BF16), (LANES, BF16), (LANES, BF16), (8 * LANES, BF16), (8 * LANES, BF16), (512, BF16)]
    return pl.pallas_call(
        _pre_attn_kernel,
        grid=(b, nt),
        in_specs=[row(d),
                  pl.BlockSpec((1, 1, 6, d), lambda i, t: (i, jnp.minimum(t, 1), 0, 0)),
                  _const_spec(n1.shape), _const_spec(win.shape), _const_spec(gq.shape), _const_spec(gk.shape),
                  _const_spec(gqa.shape), _const_spec(gkva.shape), _const_spec(wqb.shape), _const_spec(wk.shape),
                  _const_spec(wv.shape), _const_spec(gmq.shape), _const_spec(gmk.shape), tab, tab, tab, tab],
        out_specs=[row(w) for w, _ in outs],
        out_shape=[jax.ShapeDtypeStruct((b, n, w), dt) for w, dt in outs],
        compiler_params=_cparams(("parallel", "arbitrary")),
    )(xj, mods, n1, win, gq, gk, gqa, gkva, wqb, wk, wv, gmq, gmk, cg, sg, cm, sm)


def _attn_kernel(q_ref, k_ref, v_ref, o_ref, *, shared_k, ctx_len):
    t = pl.program_id(2)
    lane = lax.broadcasted_iota(jnp.int32, o_ref.shape[1:], 1)

    def run(nk):
        outs = []
        for idx in range(2):
            q = q_ref[0, :, LANES * idx:LANES * (idx + 1)]
            koff = 0 if shared_k else LANES * idx
            k = k_ref[0, :nk, koff:koff + LANES]
            s = lax.dot_general(q, k, (((1,), (1,)), ((), ())), preferred_element_type=F32)
            m = jnp.max(s, axis=-1, keepdims=True)
            p = jnp.exp2(s - m)
            l = jnp.sum(p, axis=-1, keepdims=True)
            o = jnp.dot(p.astype(BF16), v_ref[0, :nk, :], preferred_element_type=F32)
            outs.append(o / l)
        o_ref[0] = jnp.where(lane < HEAD_DIM, outs[0], outs[1]).astype(o_ref.dtype)

    @pl.when(t == 0)
    def _():
        run(ctx_len)

    @pl.when(t > 0)
    def _():
        run(k_ref.shape[1])


def _attention(q, k, v, *, shared_k, ctx_len):
    b, n, _ = q.shape
    nt = n // ROW_TILE
    pairs = q.shape[2] // (2 * LANES)
    kw = LANES if shared_k else 2 * LANES
    kmap = (lambda i, j, t: (i, 0, 0)) if shared_k else (lambda i, j, t: (i, 0, j))
    return pl.pallas_call(
        functools.partial(_attn_kernel, shared_k=shared_k, ctx_len=ctx_len),
        grid=(b, pairs, nt),
        in_specs=[pl.BlockSpec((1, ROW_TILE, 2 * LANES), lambda i, j, t: (i, t, j)),
                  pl.BlockSpec((1, n, kw), kmap),
                  pl.BlockSpec((1, n, LANES), kmap)],
        out_specs=pl.BlockSpec((1, ROW_TILE, LANES), lambda i, j, t: (i, t, j)),
        out_shape=jax.ShapeDtypeStruct((b, n, pairs * LANES), BF16),
        compiler_params=_cparams(("parallel", "parallel", "arbitrary")),
    )(q, k, v)


def _swiglu_tail(x1, mod_ref, n2_ref, w13_ref, w2_ref):
    shift, scale, gate = mod_ref[0, 0, 3:4, :], mod_ref[0, 0, 4:5, :], mod_ref[0, 0, 5:6, :]
    f_in = _rms(x1) * n2_ref[...] * (1.0 + scale) + shift
    h = jnp.dot(f_in.astype(BF16), w13_ref[...], preferred_element_type=F32)
    hid = h.shape[1] // 2
    gte, up = h[:, :hid], h[:, hid:]
    a = (gte * jax.nn.sigmoid(gte) * up).astype(BF16)
    return x1 + gate * jnp.dot(a, w2_ref[...], preferred_element_type=F32)


def _post_attn_kernel(x_ref, oa_ref, ob_ref, mod_ref, wo_ref, n2_ref, w13_ref, w2_ref, y_ref):
    half = oa_ref.shape[2]
    o = jnp.dot(oa_ref[0], wo_ref[:half, :], preferred_element_type=F32)
    o = o + jnp.dot(ob_ref[0], wo_ref[half:, :], preferred_element_type=F32)
    x1 = x_ref[0] + mod_ref[0, 0, 2:3, :] * o
    y_ref[0] = _swiglu_tail(x1, mod_ref, n2_ref, w13_ref, w2_ref)


def _post_attn(xj, oa, ob, mods, wo, n2, w13, w2):
    b, n, d = xj.shape
    nt = n // ROW_TILE
    row = lambda w: pl.BlockSpec((1, ROW_TILE, w), lambda i, t: (i, t, 0))
    return pl.pallas_call(
        _post_attn_kernel,
        grid=(b, nt),
        in_specs=[row(d), row(oa.shape[2]), row(ob.shape[2]),
                  pl.BlockSpec((1, 1, 6, d), lambda i, t: (i, jnp.minimum(t, 1), 0, 0)),
                  _const_spec(wo.shape), _const_spec(n2.shape), _const_spec(w13.shape), _const_spec(w2.shape)],
        out_specs=row(d),
        out_shape=jax.ShapeDtypeStruct((b, n, d), F32),
        compiler_params=_cparams(("parallel", "arbitrary")),
    )(xj, oa, ob, mods, wo, n2, w13, w2)


HALO = 8
SSD_DIN = 1024
SSD_NH = 16
SSD_NG = 2
SSD_NS = 128
SSD_P = 64
SSD_CONV = 5
XBC = SSD_DIN + 2 * SSD_NG * SSD_NS
RET_H = 8
RET_DK = 64
RET_DV = 128
C_Z, C_XBC, C_RQ, C_RK, C_RV, C_RG, C_DT, C_END = 0, 1024, 2560, 3072, 3584, 4608, 5632, 5760


def _softplus(x):
    return jnp.maximum(x, 0.0) + jnp.log1p(jnp.exp(-jnp.abs(x)))


def _pre_ssm_kernel(xp_ref, x_ref, xn_ref, mod_ref, n1_ref, win_ref, cw_ref, cb_ref, dtb_ref, cg_ref, sg_ref,
                    z_ref, xs_ref, bc_ref, dt_ref, rq_ref, rk_ref, rv_ref, rg_ref, ext_ref, *, ctx_tiles):
    t = pl.program_id(1)
    nt = pl.num_programs(1)
    tm = x_ref.shape[1]
    x_ext = jnp.concatenate([xp_ref[0], x_ref[0], xn_ref[0]], axis=0)
    u = _rms(x_ext) * n1_ref[...] * (1.0 + mod_ref[0, 0, 1:2, :]) + mod_ref[0, 0, 0:1, :]
    h = jnp.dot(u.astype(BF16), win_ref[...], preferred_element_type=F32)
    hm = h[HALO:HALO + tm]
    z_ref[0] = hm[:, C_Z:C_XBC]
    rg_ref[0] = hm[:, C_RG:C_DT]
    rv_ref[0] = hm[:, C_RV:C_RG].astype(BF16)
    dt_ref[0] = _softplus(hm[:, C_DT:C_END] + dtb_ref[...])

    rows = lax.broadcasted_iota(jnp.int32, (tm + 2 * HALO, 1), 0)
    has_prev = jnp.logical_and(t != 0, t != ctx_tiles)
    has_next = jnp.logical_and(t != ctx_tiles - 1, t != nt - 1)
    keep = jnp.logical_and(jnp.logical_or(rows >= HALO, has_prev), jnp.logical_or(rows < HALO + tm, has_next))
    ext_ref[...] = jnp.where(keep, h[:, C_XBC:C_RQ], 0.0)
    acc = cb_ref[...] + cw_ref[0:1, :] * ext_ref[pl.ds(HALO - 2, tm), :]
    for k in range(1, SSD_CONV):
        acc = acc + cw_ref[k:k + 1, :] * ext_ref[pl.ds(HALO - 2 + k, tm), :]
    act = acc * jax.nn.sigmoid(acc)
    xs_ref[0] = act[:, :SSD_DIN]
    bc_ref[0] = act[:, SSD_DIN:].astype(BF16)

    cg, sg = cg_ref[...], sg_ref[...]
    for j in range(RET_H * RET_DK // LANES):
        sl = slice(LANES * j, LANES * (j + 1))
        rq_ref[0, :, sl] = _rope(hm[:, C_RQ + LANES * j:C_RQ + LANES * (j + 1)], cg, sg).astype(BF16)
        rk = _rope(hm[:, C_RK + LANES * j:C_RK + LANES * (j + 1)], cg, sg) * (RET_DK ** -0.5)
        rk_ref[0, :, sl] = rk.astype(BF16)


def _pre_ssm(xj, mods, n1, win, cw, cb, dtb, cg, sg, ctx_len):
    b, n, d = xj.shape
    nt = n // ROW_TILE
    per = ROW_TILE // HALO
    row = lambda w: pl.BlockSpec((1, ROW_TILE, w), lambda i, t: (i, t, 0))
    tab = pl.BlockSpec((ROW_TILE, LANES), lambda i, t: (t, 0))
    outs = [(SSD_DIN, F32), (SSD_DIN, F32), (2 * SSD_NG * SSD_NS, BF16), (LANES, F32), (RET_H * RET_DK, BF16),
            (RET_H * RET_DK, BF16), (RET_H * RET_DV, BF16), (RET_H * RET_DV, F32)]
    return pl.pallas_call(
        functools.partial(_pre_ssm_kernel, ctx_tiles=ctx_len // ROW_TILE),
        grid=(b, nt),
        in_specs=[pl.BlockSpec((1, HALO, d), lambda i, t: (i, jnp.maximum(t * per - 1, 0), 0)),
                  row(d),
                  pl.BlockSpec((1, HALO, d), lambda i, t: (i, jnp.minimum((t + 1) * per, n // HALO - 1), 0)),
                  pl.BlockSpec((1, 1, 6, d), lambda i, t: (i, jnp.minimum(t, 1), 0, 0)),
                  _const_spec(n1.shape), _const_spec(win.shape), _const_spec(cw.shape), _const_spec(cb.shape),
                  _const_spec(dtb.shape), tab, tab],
        out_specs=[row(w) for w, _ in outs],
        out_shape=[jax.ShapeDtypeStruct((b, n, w), dt) for w, dt in outs],
        scratch_shapes=[pltpu.VMEM((ROW_TILE + 2 * HALO, XBC), F32)],
        compiler_params=_cparams(("parallel", "arbitrary")),
    )(xj, xj, xj, mods, n1, win, cw, cb, dtb, cg, sg)


def _split_dot_rhs(m, x, parts):
    acc = None
    for _ in range(parts):
        p = x.astype(BF16)
        d = jnp.dot(m, p, preferred_element_type=F32)
        acc = d if acc is None else acc + d
        x = x - p.astype(F32)
    return acc


def _split_dot_lhs(x, m, parts):
    acc = None
    for _ in range(parts):
        p = x.astype(BF16)
        d = jnp.dot(p, m, preferred_element_type=F32)
        acc = d if acc is None else acc + d
        x = x - p.astype(F32)
    return acc


def _bwd_chunk(s, nchunks, ctx_chunks):
    return jnp.where(s < ctx_chunks, ctx_chunks - 1 - s, nchunks - 1 - (s - ctx_chunks))


def _ssd_kernel(xsf_ref, bcf_ref, dtf_ref, xsb_ref, bcb_ref, dtb_ref, alog_ref, yf_ref, yb_ref, st_ref):
    @pl.when(pl.program_id(1) == 0)
    def _():
        st_ref[...] = jnp.zeros_like(st_ref)

    q = SCAN_CHUNK
    rows = lax.broadcasted_iota(jnp.int32, (q, q), 0)
    cols = lax.broadcasted_iota(jnp.int32, (q, q), 1)
    lo = cols < SSD_P
    a_row = -jnp.exp(alog_ref[...])
    gw = SSD_DIN // SSD_NG
    e_row = lax.broadcasted_iota(jnp.int32, (LANES, SSD_DIN), 0)
    e_col = lax.broadcasted_iota(jnp.int32, (LANES, SSD_DIN), 1) // SSD_P

    for d, (xs_ref, bc_ref, dt_ref, y_ref) in enumerate(((xsf_ref, bcf_ref, dtf_ref, yf_ref),
                                                          (xsb_ref, bcb_ref, dtb_ref, yb_ref))):
        xs = xs_ref[0]
        dt = dt_ref[0]
        la = dt * a_row
        tri = (cols <= rows) if d == 0 else (cols >= rows)
        c = _split_dot_rhs(tri.astype(BF16), la, 3)
        ctot = jnp.sum(la, axis=0, keepdims=True)
        de = jnp.exp(ctot - c)
        expand = (e_col + SSD_NH * d == e_row).astype(BF16)
        stacked = jnp.concatenate([dt, dt * de, jnp.exp(c), jnp.broadcast_to(jnp.exp(ctot), (8, LANES))], axis=0)
        ex = _split_dot_lhs(stacked, expand, 2)
        v = (xs * ex[0:q]).astype(BF16)
        vde = (xs * ex[q:2 * q]).astype(BF16)
        ec_x = ex[2 * q:3 * q]
        cd_x = ex[3 * q:3 * q + 1]
        c_t = c.T
        for g in range(SSD_NG):
            bg = bc_ref[0, :, SSD_NS * g:SSD_NS * (g + 1)]
            cgp = bc_ref[0, :, SSD_NG * SSD_NS + SSD_NS * g:SSD_NG * SSD_NS + SSD_NS * (g + 1)]
            sc = lax.dot_general(cgp, bg, (((1,), (1,)), ((), ())), preferred_element_type=F32)
            bg_t = bg.astype(F32).T.astype(BF16)
            st_prev = st_ref[d, g]
            yoff = jnp.dot(cgp, st_prev.astype(BF16), preferred_element_type=F32)
            st_ref[d, g] = st_prev * cd_x[:, gw * g:gw * (g + 1)] + jnp.dot(
                bg_t, vde[:, gw * g:gw * (g + 1)], preferred_element_type=F32)
            for pr in range(gw // LANES):
                sl = slice(gw * g + LANES * pr, gw * g + LANES * (pr + 1))
                outs = []
                for e in range(2):
                    ln = SSD_NH * d + (SSD_NH // SSD_NG) * g + 2 * pr + e
                    diff = c[:, ln:ln + 1] - c_t[ln:ln + 1, :]
                    m = (sc * jnp.exp(jnp.where(tri, diff, -jnp.inf))).astype(BF16)
                    outs.append(jnp.dot(m, v[:, sl], preferred_element_type=F32))
                y_ref[0, :, sl] = jnp.where(lo, outs[0], outs[1]) + yoff[:, LANES * pr:LANES * (pr + 1)] * ec_x[:, sl]


def _ssd_scan(xs, bc, dt, alog, ctx_len):
    b, n, _ = xs.shape
    nc = n // SCAN_CHUNK
    cc = ctx_len // SCAN_CHUNK
    fwd = lambda w: pl.BlockSpec((1, SCAN_CHUNK, w), lambda i, s: (i, s, 0))
    bwd = lambda w: pl.BlockSpec((1, SCAN_CHUNK, w), lambda i, s: (i, _bwd_chunk(s, nc, cc), 0))
    return pl.pallas_call(
        _ssd_kernel,
        grid=(b, nc),
        in_specs=[fwd(xs.shape[2]), fwd(bc.shape[2]), fwd(dt.shape[2]),
                  bwd(xs.shape[2]), bwd(bc.shape[2]), bwd(dt.shape[2]), _const_spec(alog.shape)],
        out_specs=[fwd(SSD_DIN), bwd(SSD_DIN)],
        out_shape=[jax.ShapeDtypeStruct((b, n, SSD_DIN), F32)] * 2,
        scratch_shapes=[pltpu.VMEM((2, SSD_NG, SSD_NS, SSD_DIN // SSD_NG), F32)],
        compiler_params=_cparams(("parallel", "arbitrary")),
    )(xs, bc, dt, xs, bc, dt, alog)


def _ret_kernel(qf_ref, kf_ref, vf_ref, qb_ref, kb_ref, vb_ref, logit_ref, yf_ref, yb_ref, st_ref):
    @pl.when(pl.program_id(1) == 0)
    def _():
        st_ref[...] = jnp.zeros_like(st_ref)

    q = SCAN_CHUNK
    rows = lax.broadcasted_iota(jnp.int32, (q, q), 0)
    cols = lax.broadcasted_iota(jnp.int32, (q, q), 1)
    lo = cols < RET_DK
    lg_row = -_softplus(-logit_ref[...])
    pos = lax.broadcasted_iota(jnp.int32, (q, 1), 0)

    for d, (q_ref, k_ref, v_ref, y_ref) in enumerate(((qf_ref, kf_ref, vf_ref, yf_ref),
                                                       (qb_ref, kb_ref, vb_ref, yb_ref))):
        tri = (cols <= rows) if d == 0 else (cols >= rows)
        dist = jnp.abs(rows - cols).astype(F32)
        steps_in = (pos + 1 if d == 0 else q - pos).astype(F32)
        steps_out = (q - 1 - pos if d == 0 else pos).astype(F32)
        for pr in range(RET_H // 2):
            qp = q_ref[0, :, LANES * pr:LANES * (pr + 1)]
            kp = k_ref[0, :, LANES * pr:LANES * (pr + 1)]
            kp_t = kp.astype(F32).T.astype(BF16)
            for e in range(2):
                hh = 2 * pr + e
                lg = lg_row[:, RET_H * d + hh:RET_H * d + hh + 1]
                qh = jnp.where(lo if e == 0 else jnp.logical_not(lo), qp, jnp.zeros_like(qp))
                sc = lax.dot_general(qh, kp, (((1,), (1,)), ((), ())), preferred_element_type=F32)
                m = (sc * jnp.exp(jnp.where(tri, dist * lg, -jnp.inf))).astype(BF16)
                vh = v_ref[0, :, RET_DV * hh:RET_DV * (hh + 1)]
                st_prev = st_ref[d, hh]
                yoff = jnp.dot(qh, st_prev.astype(BF16), preferred_element_type=F32) * jnp.exp(steps_in * lg)
                y_ref[0, :, RET_DV * hh:RET_DV * (hh + 1)] = jnp.dot(m, vh, preferred_element_type=F32) + yoff
                vde = (vh.astype(F32) * jnp.exp(steps_out * lg)).astype(BF16)
                st_ref[d, hh] = st_prev * jnp.exp(q * lg) + jnp.dot(kp_t, vde, preferred_element_type=F32)


def _ret_scan(rq, rk, rv, logit, ctx_len):
    b, n, _ = rq.shape
    nc = n // SCAN_CHUNK
    cc = ctx_len // SCAN_CHUNK
    fwd = lambda w: pl.BlockSpec((1, SCAN_CHUNK, w), lambda i, s: (i, s, 0))
    bwd = lambda w: pl.BlockSpec((1, SCAN_CHUNK, w), lambda i, s: (i, _bwd_chunk(s, nc, cc), 0))
    return pl.pallas_call(
        _ret_kernel,
        grid=(b, nc),
        in_specs=[fwd(rq.shape[2]), fwd(rk.shape[2]), fwd(rv.shape[2]),
                  bwd(rq.shape[2]), bwd(rk.shape[2]), bwd(rv.shape[2]), _const_spec(logit.shape)],
        out_specs=[fwd(RET_H * RET_DV), bwd(RET_H * RET_DV)],
        out_shape=[jax.ShapeDtypeStruct((b, n, RET_H * RET_DV), F32)] * 2,
        scratch_shapes=[pltpu.VMEM((2, RET_H, LANES, RET_DV), F32)],
        compiler_params=_cparams(("parallel", "arbitrary")),
    )(rq, rk, rv, rq, rk, rv, logit)


def _post_ssm_kernel(x_ref, yf_ref, yb_ref, xs_ref, z_ref, rf_ref, rb_ref, rg_ref, mod_ref, dsk_ref, sn_ref, rn_ref,
                     wo_ref, n2_ref, w13_ref, w2_ref, o_ref):
    z = z_ref[0]
    y = (yf_ref[0] + yb_ref[0] + dsk_ref[...] * xs_ref[0]) * (z * jax.nn.sigmoid(z))
    gw = SSD_DIN // SSD_NG
    o = None
    for g in range(SSD_NG):
        yg = (_rms(y[:, gw * g:gw * (g + 1)]) * sn_ref[:, gw * g:gw * (g + 1)]).astype(BF16)
        part = jnp.dot(yg, wo_ref[gw * g:gw * (g + 1), :], preferred_element_type=F32)
        o = part if o is None else o + part
    rg = rg_ref[0]
    gate = rg * jax.nn.sigmoid(rg)
    for hh in range(RET_H):
        sl = slice(RET_DV * hh, RET_DV * (hh + 1))
        yr = rf_ref[0, :, sl] + rb_ref[0, :, sl]
        mu = jnp.mean(yr, axis=-1, keepdims=True)
        dev = yr - mu
        var = jnp.mean(dev * dev, axis=-1, keepdims=True)
        yn = (dev * lax.rsqrt(var + EPS) * rn_ref[:, sl] * gate[:, sl]).astype(BF16)
        o = o + jnp.dot(yn, wo_ref[SSD_DIN + RET_DV * hh:SSD_DIN + RET_DV * (hh + 1), :], preferred_element_type=F32)
    x1 = x_ref[0] + mod_ref[0, 0, 2:3, :] * o
    o_ref[0] = _swiglu_tail(x1, mod_ref, n2_ref, w13_ref, w2_ref)


def _post_ssm(xj, yf, yb, xs, z, rf, rb, rg, mods, dsk, sn, rn, wo, n2, w13, w2, ctx_len):
    b, n, d = xj.shape
    off = ctx_len // ROW_TILE
    nt = n // ROW_TILE - off
    lat = lambda w: pl.BlockSpec((1, ROW_TILE, w), lambda i, t: (i, t + off, 0))
    return pl.pallas_call(
        _post_ssm_kernel,
        grid=(b, nt),
        in_specs=[lat(d)] * 8 + [pl.BlockSpec((1, 1, 6, d), lambda i, t: (i, 1, 0, 0)),
                                 _const_spec(dsk.shape), _const_spec(sn.shape), _const_spec(rn.shape),
                                 _const_spec(wo.shape), _const_spec(n2.shape), _const_spec(w13.shape),
                                 _const_spec(w2.shape)],
        out_specs=pl.BlockSpec((1, ROW_TILE, d), lambda i, t: (i, t, 0)),
        out_shape=jax.ShapeDtypeStruct((b, nt * ROW_TILE, d), F32),
        compiler_params=_cparams(("parallel", "arbitrary")),
    )(xj, yf, yb, xs, z, rf, rb, rg, mods, dsk, sn, rn, wo, n2, w13, w2)


def _ssm_layer_params(w_in, conv_w, conv_b, dt_bias, a_log, d_skip, ssd_norm, ret_logit, ret_norm, w_out):
    z, xbc, dt, rq, rk, rv, rg = jnp.split(
        w_in, np.cumsum([SSD_DIN, XBC, 2 * SSD_NH, RET_H * RET_DK, RET_H * RET_DK, RET_H * RET_DV]).tolist(), axis=1)
    dt = jnp.pad(dt, ((0, 0), (0, LANES - 2 * SSD_NH)))
    win = jnp.concatenate([z, xbc, rq, rk, rv, rg, dt], axis=1).astype(BF16)
    pad_row = lambda a: jnp.pad(a.reshape(-1), (0, LANES - a.size)).reshape(1, LANES)
    return dict(win=win, cw=conv_w, cb=conv_b.reshape(1, -1), dtb=pad_row(dt_bias), alog=pad_row(a_log),
                dsk=jnp.repeat(d_skip, SSD_P).reshape(1, -1), sn=ssd_norm.reshape(1, -1), logit=pad_row(ret_logit),
                rn=ret_norm.reshape(1, -1), wo=w_out.astype(BF16))


def _ssm_layer(xj, mods, prm, tabs, n1, n2, w13, w2, ctx_len):
    z, xs, bc, dt, rq, rk, rv, rg = _pre_ssm(xj, mods, n1, prm['win'], prm['cw'], prm['cb'], prm['dtb'],
                                             tabs[0], tabs[1], ctx_len)
    yf, yb = _ssd_scan(xs, bc, dt, prm['alog'], ctx_len)
    rf, rb = _ret_scan(rq, rk, rv, prm['logit'], ctx_len)
    return _post_ssm(xj, yf, yb, xs, z, rf, rb, rg, mods, prm['dsk'], prm['sn'], prm['rn'], prm['wo'], n2, w13, w2,
                     ctx_len)


def _rope_angles(seq_len, ctx_len, dim):
    rows = seq_len // GRID_W
    rr, cc = jnp.meshgrid(jnp.arange(rows, dtype=F32), jnp.arange(GRID_W, dtype=F32), indexing='ij')
    quarter = dim // 4
    inv = ROPE_THETA ** (-jnp.arange(quarter, dtype=F32) / quarter)
    ang = jnp.concatenate([rr.reshape(-1)[:, None] * inv, cc.reshape(-1)[:, None] * inv], axis=-1)
    cos = jnp.concatenate([jnp.ones((ctx_len, dim // 2), F32), jnp.cos(ang)], axis=0)
    sin = jnp.concatenate([jnp.zeros((ctx_len, dim // 2), F32), jnp.sin(ang)], axis=0)
    return cos, sin


def _interleaved_tables(cos, sin):
    c = jnp.repeat(cos, 2, axis=-1)
    s = jnp.stack([-sin, sin], axis=-1).reshape(sin.shape[0], -1)
    return c, s


def _attn_layer_params(w_in, gqa_qn, gqa_kn, mla_qa_norm, mla_wq_b, mla_kva_norm, mla_wkv_b, mla_qn, mla_kn, w_out):
    d = w_in.shape[0]
    qa, ka, va, ql, kvl, kr = jnp.split(w_in, np.cumsum([512, 128, 128, MLA_Q_RANK, MLA_KV_RANK]).tolist(), axis=1)
    qa = qa.reshape(d, GQA_HQ, HEAD_DIM)
    qa = jnp.concatenate([jnp.concatenate([qa[:, j], qa[:, j + 4]], axis=1) for j in range(4)], axis=1)
    kr = jnp.pad(kr, ((0, 0), (MLA_DNOPE, LANES - MLA_DQK)))
    win = jnp.concatenate([qa, ka, va, ql, kvl, kr], axis=1).astype(BF16)
    wqb = jnp.pad(mla_wq_b.reshape(MLA_Q_RANK, MLA_H, MLA_DQK), ((0, 0), (0, 0), (0, LANES - MLA_DQK)))
    wqb = wqb.reshape(MLA_Q_RANK, MLA_H * LANES).astype(BF16)
    wkv = mla_wkv_b.reshape(MLA_KV_RANK, MLA_H, MLA_DNOPE + HEAD_DIM)
    wk = jnp.pad(wkv[:, :, :MLA_DNOPE], ((0, 0), (0, 0), (0, LANES - MLA_DNOPE))).reshape(MLA_KV_RANK, MLA_H * LANES)
    wv = wkv[:, :, MLA_DNOPE:].reshape(MLA_KV_RANK, MLA_H * HEAD_DIM)
    pad_gain = lambda g: jnp.pad(g, (0, LANES - MLA_DQK)).reshape(1, LANES)
    woa = w_out[:512].reshape(GQA_HQ, HEAD_DIM, d)
    woa = jnp.concatenate([jnp.concatenate([woa[j], woa[j + 4]], axis=0) for j in range(4)], axis=0)
    wo = jnp.concatenate([woa, w_out[512:]], axis=0).astype(BF16)
    return dict(win=win, gq=jnp.tile(gqa_qn, 2).reshape(1, LANES), gk=jnp.tile(gqa_kn, 2).reshape(1, LANES),
                gqa=mla_qa_norm.reshape(1, -1), gkva=mla_kva_norm.reshape(1, -1), wqb=wqb, wk=wk.astype(BF16),
                wv=wv.astype(BF16), gmq=pad_gain(mla_qn), gmk=pad_gain(mla_kn), wo=wo)


def _attn_tables(seq_len, ctx_len):
    cg, sg = _interleaved_tables(*_rope_angles(seq_len, ctx_len, HEAD_DIM))
    cg, sg = jnp.tile(cg, (1, 2)), jnp.tile(sg, (1, 2))
    cm, sm = _interleaved_tables(*_rope_angles(seq_len, ctx_len, MLA_DROPE))
    n = cm.shape[0]
    cm = jnp.concatenate([jnp.ones((n, MLA_DNOPE), F32), cm, jnp.ones((n, LANES - MLA_DQK), F32)], axis=1)
    sm = jnp.pad(sm, ((0, 0), (MLA_DNOPE, LANES - MLA_DQK)))
    return cg, sg, cm, sm


def _layer_mods(mods_i, batch):
    d = mods_i.shape[1] // 6
    lat = mods_i[:batch].reshape(batch, 1, 6, d)
    ctx = jnp.broadcast_to(mods_i[batch].reshape(1, 1, 6, d), (batch, 1, 6, d))
    return jnp.concatenate([ctx, lat], axis=1)


def _attention_layer(xj, mods, prm, tabs, n1, n2, w13, w2, ctx_len):
    qg, kg, vg, qm, km, vm = _pre_attn(xj, mods, n1, prm['win'], prm['gq'], prm['gk'], prm['gqa'], prm['gkva'],
                                       prm['wqb'], prm['wk'], prm['wv'], prm['gmq'], prm['gmk'], *tabs)
    oa = _attention(qg, kg, vg, shared_k=True, ctx_len=ctx_len)
    ob = _attention(qm, km, vm, shared_k=False, ctx_len=ctx_len)
    return _post_attn(xj, oa, ob, mods, prm['wo'], n2, w13, w2)


def _all_mods(c, c_ctx, mod_w, mod_b):
    batch = c.shape[0]
    rows = -(-(batch + 1) // 8) * 8
    cvec = jnp.concatenate([c, c_ctx[None, :], jnp.zeros((rows - batch - 1, c.shape[1]), F32)], axis=0)
    mods = _modulation(cvec, mod_w, mod_b)
    return [_layer_mods(mods[i], batch) for i in range(mod_w.shape[0])]


def kernel(x, c, ctx, c_ctx, mod_w, mod_b, norm1_w, norm2_w, ffn_w13, ffn_w2, attn_w_in, gqa_qn, gqa_kn, mla_qa_norm, mla_wq_b, mla_kva_norm, mla_wkv_b, mla_qn, mla_kn, attn_w_out, ssm_w_in, ssd_conv_w, ssd_conv_b, ssd_dt_bias, ssd_a_log, ssd_d, ssd_norm, ret_decay_logit, ret_norm, ssm_w_out):
    ctx_len, seq_len = ctx.shape[1], x.shape[1]
    assert ctx_len % ROW_TILE == 0 and seq_len % ROW_TILE == 0 and seq_len % GRID_W == 0
    assert mod_w.shape[0] == 2, "one attention layer followed by one SSM layer"
    mods = _all_mods(c, c_ctx, mod_w, mod_b)
    tabs = _attn_tables(seq_len, ctx_len)
    xj = jnp.concatenate([ctx, x], axis=1)
    prm0 = _attn_layer_params(attn_w_in[0], gqa_qn[0], gqa_kn[0], mla_qa_norm[0], mla_wq_b[0], mla_kva_norm[0],
                              mla_wkv_b[0], mla_qn[0], mla_kn[0], attn_w_out[0])
    xj = _attention_layer(xj, mods[0], prm0, tabs, norm1_w[0].reshape(1, -1), norm2_w[0].reshape(1, -1),
                          ffn_w13[0].astype(BF16), ffn_w2[0].astype(BF16), ctx_len)
    prm1 = _ssm_layer_params(ssm_w_in[0], ssd_conv_w[0], ssd_conv_b[0], ssd_dt_bias[0], ssd_a_log[0], ssd_d[0],
                             ssd_norm[0], ret_decay_logit[0], ret_norm[0], ssm_w_out[0])
    return _ssm_layer(xj, mods[1], prm1, tabs, norm1_w[1].reshape(1, -1), norm2_w[1].reshape(1, -1),
                      ffn_w13[1].astype(BF16), ffn_w2[1].astype(BF16), ctx_len)


def _debug_layer1(x, c, ctx, c_ctx, mod_w, mod_b, norm1_w, norm2_w, ffn_w13, ffn_w2, ssm_w_in, ssd_conv_w, ssd_conv_b,
                  ssd_dt_bias, ssd_a_log, ssd_d, ssd_norm, ret_decay_logit, ret_norm, ssm_w_out, **_):
    ctx_len, seq_len = ctx.shape[1], x.shape[1]
    mods = _all_mods(c, c_ctx, mod_w, mod_b)
    tabs = _attn_tables(seq_len, ctx_len)
    xj = jnp.concatenate([ctx, x], axis=1)
    prm1 = _ssm_layer_params(ssm_w_in[0], ssd_conv_w[0], ssd_conv_b[0], ssd_dt_bias[0], ssd_a_log[0], ssd_d[0],
                             ssd_norm[0], ret_decay_logit[0], ret_norm[0], ssm_w_out[0])
    return _ssm_layer(xj, mods[1], prm1, tabs, norm1_w[1].reshape(1, -1), norm2_w[1].reshape(1, -1),
                      ffn_w13[1].astype(BF16), ffn_w2[1].astype(BF16), ctx_len)


def _debug_layer0(x, c, ctx, c_ctx, mod_w, mod_b, norm1_w, norm2_w, ffn_w13, ffn_w2, attn_w_in, gqa_qn, gqa_kn,
                  mla_qa_norm, mla_wq_b, mla_kva_norm, mla_wkv_b, mla_qn, mla_kn, attn_w_out, **_):
    ctx_len, seq_len = ctx.shape[1], x.shape[1]
    mods = _all_mods(c, c_ctx, mod_w, mod_b)
    xj = jnp.concatenate([ctx, x], axis=1)
    prm = _attn_layer_params(attn_w_in[0], gqa_qn[0], gqa_kn[0], mla_qa_norm[0], mla_wq_b[0], mla_kva_norm[0],
                             mla_wkv_b[0], mla_qn[0], mla_kn[0], attn_w_out[0])
    return _attention_layer(xj, mods[0], prm, _attn_tables(seq_len, ctx_len), norm1_w[0].reshape(1, -1),
                            norm2_w[0].reshape(1, -1), ffn_w13[0].astype(BF16), ffn_w2[0].astype(BF16), ctx_len)
```

```python
import functools
import math

import jax
import jax.numpy as jnp
import numpy as np
from jax import lax
from jax.experimental import pallas as pl
from jax.experimental.pallas import tpu as pltpu

F32 = jnp.float32
BF16 = jnp.bfloat16

EPS = 1e-6
ROPE_THETA = 10000.0
GRID_W = 64
HEAD_DIM = 64
LANES = 128
ROW_TILE = 256
SCAN_CHUNK = 128
ROW_BLOCK = 32
KEY_TILE = 256
VMEM_LIMIT = 56 * 1024 * 1024

GQA_HQ = 8
GQA_HKV = 2
MLA_H = 8
MLA_DNOPE = 64
MLA_DROPE = 32
MLA_DQK = MLA_DNOPE + MLA_DROPE
MLA_Q_RANK = 384
MLA_KV_RANK = 256
LOG2E = 1.4426950408889634


def _cparams(sem):
    return pltpu.CompilerParams(dimension_semantics=sem, vmem_limit_bytes=VMEM_LIMIT)


def _const_spec(shape):
    nd = len(shape)
    return pl.BlockSpec(shape, lambda *_: (0,) * nd, pipeline_mode=pl.Buffered(1))


def _rms(x):
    return x * lax.rsqrt(jnp.mean(x * x, axis=-1, keepdims=True) + EPS)


def _swap_pairs(y):
    lane = lax.broadcasted_iota(jnp.int32, y.shape, y.ndim - 1)
    nxt = pltpu.roll(y, LANES - 1, y.ndim - 1)
    prv = pltpu.roll(y, 1, y.ndim - 1)
    return jnp.where(lane % 2 == 0, nxt, prv)


def _rope(y, cos, sin_signed):
    return y * cos + _swap_pairs(y) * sin_signed


def _mod_kernel(c_ref, w_ref, b_ref, o_ref):
    c = c_ref[...]
    a = (c * jax.nn.sigmoid(c)).astype(BF16)
    o_ref[0] = jnp.dot(a, w_ref[0].astype(BF16), preferred_element_type=F32) + b_ref[0]


def _modulation(cvec, mod_w, mod_b):
    depth, d, n = mod_w.shape
    rows = cvec.shape[0]
    tn = 512
    return pl.pallas_call(
        _mod_kernel,
        grid=(depth, n // tn),
        in_specs=[pl.BlockSpec((rows, d), lambda i, j: (0, 0)),
                  pl.BlockSpec((1, d, tn), lambda i, j: (i, 0, j)),
                  pl.BlockSpec((1, 1, tn), lambda i, j: (i, 0, j))],
        out_specs=pl.BlockSpec((1, rows, tn), lambda i, j: (i, 0, j)),
        out_shape=jax.ShapeDtypeStruct((depth, rows, n), F32),
        compiler_params=_cparams(("arbitrary", "arbitrary")), name="modulation",
    )(cvec, mod_w, mod_b.reshape(depth, 1, n))


def _pre_attn_kernel(x_ref, mod_ref, n1_ref, win_ref, gq_ref, gk_ref, gqa_ref, gkva_ref, wqb_ref, wk_ref, wv_ref,
                     gmq_ref, gmk_ref, cg_ref, sg_ref, cm_ref, sm_ref,
                     qg_ref, kg_ref, vglo_ref, vghi_ref, qm_ref, km_ref, vmlo_ref, vmhi_ref):
    x = x_ref[0]
    shift = mod_ref[0, 0, 0:1, :]
    scale = mod_ref[0, 0, 1:2, :]
    u = _rms(x) * n1_ref[...] * (1.0 + scale) + shift
    h = jnp.dot(u.astype(BF16), win_ref[...], preferred_element_type=F32)

    tm = x.shape[0]
    lane = lax.broadcasted_iota(jnp.int32, (tm, LANES), 1)
    lo = lane < HEAD_DIM
    cg, sg, cm, sm = cg_ref[...], sg_ref[...], cm_ref[...], sm_ref[...]

    def pair_norm_rope(blk, gain):
        sq = blk * blk
        ss_lo = jnp.sum(jnp.where(lo, sq, 0.0), axis=-1, keepdims=True)
        ss_hi = jnp.sum(jnp.where(lo, 0.0, sq), axis=-1, keepdims=True)
        r = lax.rsqrt(jnp.where(lo, ss_lo, ss_hi) * (1.0 / HEAD_DIM) + EPS)
        return _rope(blk * r * gain, cg, sg)

    def head_norm_rope(blk, gain):
        r = lax.rsqrt(jnp.sum(blk * blk, axis=-1, keepdims=True) * (1.0 / MLA_DQK) + EPS)
        return _rope(blk * r * gain, cm, sm)

    qscale = (HEAD_DIM ** -0.5) * LOG2E
    for j in range(4):
        y = pair_norm_rope(h[:, LANES * j:LANES * (j + 1)], gq_ref[...]) * qscale
        qg_ref[0, :, LANES * (2 * j):LANES * (2 * j + 1)] = jnp.where(lo, y, 0.0).astype(BF16)
        qg_ref[0, :, LANES * (2 * j + 1):LANES * (2 * j + 2)] = jnp.where(lo, 0.0, y).astype(BF16)
    kg_ref[0] = pair_norm_rope(h[:, 512:640], gk_ref[...]).astype(BF16)

    def put_values(blk, lo_ref, hi_ref, sl):
        lo_ref[0, :, sl] = jnp.where(lo, blk, jnp.where(lane == HEAD_DIM, 1.0, 0.0)).astype(BF16)
        hi_ref[0, :, sl] = jnp.where(lo, jnp.where(lane == 0, 1.0, 0.0), blk).astype(BF16)

    put_values(h[:, 640:768], vglo_ref, vghi_ref, slice(0, LANES))

    ql = (_rms(h[:, 768:1152]) * gqa_ref[...]).astype(BF16)
    qm = jnp.dot(ql, wqb_ref[...], preferred_element_type=F32)
    kvl = (_rms(h[:, 1152:1408]) * gkva_ref[...]).astype(BF16)
    kn = jnp.dot(kvl, wk_ref[...], preferred_element_type=F32)
    vm = jnp.dot(kvl, wv_ref[...], preferred_element_type=F32)
    for j in range(MLA_H // 2):
        put_values(vm[:, LANES * j:LANES * (j + 1)], vmlo_ref, vmhi_ref, slice(LANES * j, LANES * (j + 1)))
    kr = h[:, 1408:1536]
    mscale = (MLA_DQK ** -0.5) * LOG2E
    for hh in range(MLA_H):
        sl = slice(LANES * hh, LANES * (hh + 1))
        qm_ref[0, :, sl] = (head_norm_rope(qm[:, sl], gmq_ref[...]) * mscale).astype(BF16)
        km_ref[0, :, sl] = head_norm_rope(kn[:, sl] + kr, gmk_ref[...]).astype(BF16)


def _pre_attn(xj, mods, n1, win, gq, gk, gqa, gkva, wqb, wk, wv, gmq, gmk, cg, sg, cm, sm):
    b, n, d = xj.shape
    nt = n // ROW_TILE
    row = lambda w: pl.BlockSpec((1, ROW_TILE, w), lambda i, t: (i, t, 0))
    tab = pl.BlockSpec((ROW_TILE, LANES), lambda i, t: (t, 0))
    outs = [(8 * LANES, BF16), (LANES, BF16), (LANES, BF16), (LANES, BF16),
            (8 * LANES, BF16), (8 * LANES, BF16), (4 * LANES, BF16), (4 * LANES, BF16)]
    return pl.pallas_call(
        _pre_attn_kernel,
        grid=(b, nt),
        in_specs=[row(d),
                  pl.BlockSpec((1, 1, 6, d), lambda i, t: (i, jnp.minimum(t, 1), 0, 0)),
                  _const_spec(n1.shape), _const_spec(win.shape), _const_spec(gq.shape), _const_spec(gk.shape),
                  _const_spec(gqa.shape), _const_spec(gkva.shape), _const_spec(wqb.shape), _const_spec(wk.shape),
                  _const_spec(wv.shape), _const_spec(gmq.shape), _const_spec(gmk.shape), tab, tab, tab, tab],
        out_specs=[row(w) for w, _ in outs],
        out_shape=[jax.ShapeDtypeStruct((b, n, w), dt) for w, dt in outs],
        compiler_params=_cparams(("parallel", "arbitrary")), name="pre_attn",
    )(xj, mods, n1, win, gq, gk, gqa, gkva, wqb, wk, wv, gmq, gmk, cg, sg, cm, sm)


def _attn_kernel(q_ref, k_ref, vlo_ref, vhi_ref, o_ref, *, shared_k, ctx_len):
    t = pl.program_id(2)
    lane = lax.broadcasted_iota(jnp.int32, o_ref.shape[1:], 1)

    def run(nk):
        s, acc = [], []
        for idx in range(2):
            q = q_ref[0, :, LANES * idx:LANES * (idx + 1)]
            koff = 0 if shared_k else LANES * idx
            s.append(lax.dot_general(q, k_ref[0, :nk, koff:koff + LANES], (((1,), (1,)), ((), ())),
                                     preferred_element_type=F32))
        for idx, v_ref in enumerate((vlo_ref, vhi_ref)):
            p = jnp.exp2(s[idx] - jnp.max(s[idx], axis=-1, keepdims=True)).astype(BF16)
            acc.append(jnp.dot(p, v_ref[0, :nk, :], preferred_element_type=F32))
        out0 = acc[0] / acc[0][:, HEAD_DIM:HEAD_DIM + 1]
        out1 = acc[1] / acc[1][:, 0:1]
        o_ref[0] = jnp.where(lane < HEAD_DIM, out0, out1).astype(o_ref.dtype)

    @pl.when(t == 0)
    def _():
        run(ctx_len)

    @pl.when(t > 0)
    def _():
        run(k_ref.shape[1])


def _attention(q, k, vlo, vhi, *, shared_k, ctx_len):
    b, n, _ = q.shape
    nt = n // ROW_TILE
    pairs = q.shape[2] // (2 * LANES)
    kw = LANES if shared_k else 2 * LANES
    kmap = (lambda i, j, t: (i, 0, 0)) if shared_k else (lambda i, j, t: (i, 0, j))
    return pl.pallas_call(
        functools.partial(_attn_kernel, shared_k=shared_k, ctx_len=ctx_len),
        grid=(b, pairs, nt),
        in_specs=[pl.BlockSpec((1, ROW_TILE, 2 * LANES), lambda i, j, t: (i, t, j)),
                  pl.BlockSpec((1, n, kw), kmap),
                  pl.BlockSpec((1, n, LANES), kmap),
                  pl.BlockSpec((1, n, LANES), kmap)],
        out_specs=pl.BlockSpec((1, ROW_TILE, LANES), lambda i, j, t: (i, t, j)),
        out_shape=jax.ShapeDtypeStruct((b, n, pairs * LANES), BF16),
        compiler_params=_cparams(("parallel", "parallel", "arbitrary")),
        name="attn_gqa" if shared_k else "attn_mla",
    )(q, k, vlo, vhi)


def _swiglu_tail(x1, mod_ref, n2_ref, w13_ref, w2_ref):
    shift, scale, gate = mod_ref[0, 0, 3:4, :], mod_ref[0, 0, 4:5, :], mod_ref[0, 0, 5:6, :]
    f_in = _rms(x1) * n2_ref[...] * (1.0 + scale) + shift
    h = jnp.dot(f_in.astype(BF16), w13_ref[...], preferred_element_type=F32)
    hid = h.shape[1] // 2
    gte, up = h[:, :hid], h[:, hid:]
    a = (gte * jax.nn.sigmoid(gte) * up).astype(BF16)
    return x1 + gate * jnp.dot(a, w2_ref[...], preferred_element_type=F32)


def _post_attn_kernel(x_ref, oa_ref, ob_ref, mod_ref, wo_ref, n2_ref, w13_ref, w2_ref, y_ref):
    half = oa_ref.shape[2]
    o = jnp.dot(oa_ref[0], wo_ref[:half, :], preferred_element_type=F32)
    o = o + jnp.dot(ob_ref[0], wo_ref[half:, :], preferred_element_type=F32)
    x1 = x_ref[0] + mod_ref[0, 0, 2:3, :] * o
    y_ref[0] = _swiglu_tail(x1, mod_ref, n2_ref, w13_ref, w2_ref)


def _post_attn(xj, oa, ob, mods, wo, n2, w13, w2):
    b, n, d = xj.shape
    nt = n // ROW_TILE
    row = lambda w: pl.BlockSpec((1, ROW_TILE, w), lambda i, t: (i, t, 0))
    return pl.pallas_call(
        _post_attn_kernel,
        grid=(b, nt),
        in_specs=[row(d), row(oa.shape[2]), row(ob.shape[2]),
                  pl.BlockSpec((1, 1, 6, d), lambda i, t: (i, jnp.minimum(t, 1), 0, 0)),
                  _const_spec(wo.shape), _const_spec(n2.shape), _const_spec(w13.shape), _const_spec(w2.shape)],
        out_specs=row(d),
        out_shape=jax.ShapeDtypeStruct((b, n, d), F32),
        compiler_params=_cparams(("parallel", "arbitrary")), name="post_attn",
    )(xj, oa, ob, mods, wo, n2, w13, w2)


HALO = 8
SSD_DIN = 1024
SSD_NH = 16
SSD_NG = 2
SSD_NS = 128
SSD_P = 64
SSD_CONV = 5
XBC = SSD_DIN + 2 * SSD_NG * SSD_NS
RET_H = 8
RET_DK = 64
RET_DV = 128
C_Z, C_XBC, C_RQ, C_RK, C_RV, C_RG, C_DT, C_END = 0, 1024, 2560, 3072, 3584, 4608, 5632, 5760


def _softplus(x):
    return jnp.maximum(x, 0.0) + jnp.log1p(jnp.exp(-jnp.abs(x)))


def _pre_ssm_kernel(xp_ref, x_ref, xn_ref, mod_ref, n1_ref, win_ref, cw_ref, cb_ref, dtb_ref, cg_ref, sg_ref,
                    z_ref, xs_ref, bc_ref, dt_ref, rq_ref, rk_ref, rv_ref, rg_ref, ext_ref, *, ctx_tiles):
    t = pl.program_id(1)
    nt = pl.num_programs(1)
    tm = x_ref.shape[1]
    x_ext = jnp.concatenate([xp_ref[0], x_ref[0], xn_ref[0]], axis=0)
    u = _rms(x_ext) * n1_ref[...] * (1.0 + mod_ref[0, 0, 1:2, :]) + mod_ref[0, 0, 0:1, :]
    h = jnp.dot(u.astype(BF16), win_ref[...], preferred_element_type=F32)
    hm = h[HALO:HALO + tm]
    z_ref[0] = hm[:, C_Z:C_XBC]
    rg_ref[0] = hm[:, C_RG:C_DT]
    rv_ref[0] = hm[:, C_RV:C_RG].astype(BF16)
    dt_ref[0] = _softplus(hm[:, C_DT:C_END] + dtb_ref[...])

    rows = lax.broadcasted_iota(jnp.int32, (tm + 2 * HALO, 1), 0)
    has_prev = jnp.logical_and(t != 0, t != ctx_tiles)
    has_next = jnp.logical_and(t != ctx_tiles - 1, t != nt - 1)
    keep = jnp.logical_and(jnp.logical_or(rows >= HALO, has_prev), jnp.logical_or(rows < HALO + tm, has_next))
    ext_ref[...] = jnp.where(keep, h[:, C_XBC:C_RQ], 0.0)
    acc = cb_ref[...] + cw_ref[0:1, :] * ext_ref[pl.ds(HALO - 2, tm), :]
    for k in range(1, SSD_CONV):
        acc = acc + cw_ref[k:k + 1, :] * ext_ref[pl.ds(HALO - 2 + k, tm), :]
    act = acc * jax.nn.sigmoid(acc)
    xs_ref[0] = act[:, :SSD_DIN]
    bc_ref[0] = act[:, SSD_DIN:].astype(BF16)

    cg, sg = cg_ref[...], sg_ref[...]
    for j in range(RET_H * RET_DK // LANES):
        sl = slice(LANES * j, LANES * (j + 1))
        rq_ref[0, :, sl] = _rope(hm[:, C_RQ + LANES * j:C_RQ + LANES * (j + 1)], cg, sg).astype(BF16)
        rk = _rope(hm[:, C_RK + LANES * j:C_RK + LANES * (j + 1)], cg, sg) * (RET_DK ** -0.5)
        rk_ref[0, :, sl] = rk.astype(BF16)


def _pre_ssm(xj, mods, n1, win, cw, cb, dtb, cg, sg, ctx_len):
    b, n, d = xj.shape
    nt = n // ROW_TILE
    per = ROW_TILE // HALO
    row = lambda w: pl.BlockSpec((1, ROW_TILE, w), lambda i, t: (i, t, 0))
    tab = pl.BlockSpec((ROW_TILE, LANES), lambda i, t: (t, 0))
    outs = [(SSD_DIN, F32), (SSD_DIN, F32), (2 * SSD_NG * SSD_NS, BF16), (LANES, F32), (RET_H * RET_DK, BF16),
            (RET_H * RET_DK, BF16), (RET_H * RET_DV, BF16), (RET_H * RET_DV, F32)]
    return pl.pallas_call(
        functools.partial(_pre_ssm_kernel, ctx_tiles=ctx_len // ROW_TILE),
        grid=(b, nt),
        in_specs=[pl.BlockSpec((1, HALO, d), lambda i, t: (i, jnp.maximum(t * per - 1, 0), 0)),
                  row(d),
                  pl.BlockSpec((1, HALO, d), lambda i, t: (i, jnp.minimum((t + 1) * per, n // HALO - 1), 0)),
                  pl.BlockSpec((1, 1, 6, d), lambda i, t: (i, jnp.minimum(t, 1), 0, 0)),
                  _const_spec(n1.shape), _const_spec(win.shape), _const_spec(cw.shape), _const_spec(cb.shape),
                  _const_spec(dtb.shape), tab, tab],
        out_specs=[row(w) for w, _ in outs],
        out_shape=[jax.ShapeDtypeStruct((b, n, w), dt) for w, dt in outs],
        scratch_shapes=[pltpu.VMEM((ROW_TILE + 2 * HALO, XBC), F32)],
        compiler_params=_cparams(("parallel", "arbitrary")), name="pre_ssm",
    )(xj, xj, xj, mods, n1, win, cw, cb, dtb, cg, sg)


def _split_dot_rhs(m, x, parts):
    acc = None
    for _ in range(parts):
        p = x.astype(BF16)
        d = jnp.dot(m, p, preferred_element_type=F32)
        acc = d if acc is None else acc + d
        x = x - p.astype(F32)
    return acc


def _split_dot_lhs(x, m, parts):
    acc = None
    for _ in range(parts):
        p = x.astype(BF16)
        d = jnp.dot(p, m, preferred_element_type=F32)
        acc = d if acc is None else acc + d
        x = x - p.astype(F32)
    return acc


def _running_sum_rows(x, reverse):
    n = x.shape[0]
    row = lax.broadcasted_iota(jnp.int32, x.shape, 0)
    k = 1
    while k < n:
        if reverse:
            x = x + jnp.where(row < n - k, pltpu.roll(x, n - k, 0), 0.0)
        else:
            x = x + jnp.where(row >= k, pltpu.roll(x, k, 0), 0.0)
        k *= 2
    return x


def _bwd_chunk(s, nchunks, ctx_chunks):
    return jnp.where(s < ctx_chunks, ctx_chunks - 1 - s, nchunks - 1 - (s - ctx_chunks))


def _scan_masks():
    i = np.arange(SCAN_CHUNK)
    tri = np.stack([i[None, :] <= i[:, None], i[None, :] >= i[:, None]])
    lane, col = np.arange(LANES)[:, None], np.arange(SSD_DIN)[None, :] // SSD_P
    expand = np.stack([col + SSD_NH * d == lane for d in range(2)])
    return jnp.asarray(tri, BF16), jnp.asarray(expand, BF16)


def _ssd_kernel(xsf_ref, bcf_ref, dtf_ref, xsb_ref, bcb_ref, dtb_ref, alog_ref, alogc_ref, tri_ref, exp_ref,
                yf_ref, yb_ref, st_ref):
    @pl.when(pl.program_id(1) == 0)
    def _():
        st_ref[...] = jnp.zeros_like(st_ref)

    q = SCAN_CHUNK
    rows = lax.broadcasted_iota(jnp.int32, (q, q), 0)
    cols = lax.broadcasted_iota(jnp.int32, (q, q), 1)
    lo = cols < SSD_P
    a_row = -jnp.exp(alog_ref[...])
    a_col = -jnp.exp(alogc_ref[...])
    gw = SSD_DIN // SSD_NG

    dirs = ((xsf_ref, bcf_ref, dtf_ref, yf_ref), (xsb_ref, bcb_ref, dtb_ref, yb_ref))
    tris = (cols <= rows, cols >= rows)
    cs, c_ts, vdes, xsbs, cdxs = [], [], [], [], []
    for d, (xs_ref, _, dt_ref, _) in enumerate(dirs):
        xs = xs_ref[0]
        dt = dt_ref[0]
        la = dt * a_row
        c = _running_sum_rows(la, reverse=(d == 1))
        ctot = jnp.sum(la, axis=0, keepdims=True)
        expand = exp_ref[d]
        stacked = jnp.concatenate([dt * jnp.exp(ctot - c), jnp.broadcast_to(jnp.exp(ctot), (8, LANES))], axis=0)
        ex = _split_dot_lhs(stacked, expand, 2)
        vdes.append((xs * ex[0:q]).astype(BF16))
        cdxs.append(ex[q:q + 1])
        xsbs.append(xs.astype(BF16))
        c2 = c * LOG2E
        cs.append(c2)
        c_ts.append((c2 - jnp.log2(dt)).T)

    scs, cgs, rhss = {}, {}, {}
    for g in range(SSD_NG):
        for d, (_, bc_ref, _, _) in enumerate(dirs):
            bg = bc_ref[0, :, SSD_NS * g:SSD_NS * (g + 1)]
            cgp = bc_ref[0, :, SSD_NG * SSD_NS + SSD_NS * g:SSD_NG * SSD_NS + SSD_NS * (g + 1)]
            scs[d, g] = lax.dot_general(cgp, bg, (((1,), (1,)), ((), ())), preferred_element_type=F32)
            cgs[d, g] = cgp
            st_prev = st_ref[d, g]
            st_ref[d, g] = st_prev * cdxs[d][:, gw * g:gw * (g + 1)] + jnp.dot(
                bg.astype(F32).T.astype(BF16), vdes[d][:, gw * g:gw * (g + 1)], preferred_element_type=F32)
            st_b = st_prev.astype(BF16)
            for pr in range(gw // LANES):
                rhss[d, g, pr] = jnp.concatenate([xsbs[d][:, gw * g + LANES * pr:gw * g + LANES * (pr + 1)],
                                                  st_b[:, LANES * pr:LANES * (pr + 1)]], axis=0)

    for g in range(SSD_NG):
        for pr in range(gw // LANES):
            for d, (_, _, _, y_ref) in enumerate(dirs):
                outs = []
                for e in range(2):
                    ln = SSD_NH * d + (SSD_NH // SSD_NG) * g + 2 * pr + e
                    c_i = jnp.broadcast_to(cs[d][:, ln:ln + 1], (q, q))
                    m = scs[d, g] * jnp.exp2(jnp.where(tris[d], c_i - c_ts[d][ln:ln + 1, :], -jnp.inf))
                    lhs = jnp.concatenate([m.astype(BF16), cgs[d, g] * jnp.exp2(c_i).astype(BF16)], axis=1)
                    outs.append(jnp.dot(lhs, rhss[d, g, pr], preferred_element_type=F32))
                sl = slice(gw * g + LANES * pr, gw * g + LANES * (pr + 1))
                y_ref[0, :, sl] = jnp.where(lo, outs[0], outs[1])


def _ssd_scan(xs, bc, dt, alog, ctx_len):
    b, n, _ = xs.shape
    nc = n // SCAN_CHUNK
    cc = ctx_len // SCAN_CHUNK
    fwd = lambda w: pl.BlockSpec((1, SCAN_CHUNK, w), lambda i, s: (i, s, 0))
    bwd = lambda w: pl.BlockSpec((1, SCAN_CHUNK, w), lambda i, s: (i, _bwd_chunk(s, nc, cc), 0))
    tri, expand = _scan_masks()
    return pl.pallas_call(
        _ssd_kernel,
        grid=(b, nc),
        in_specs=[fwd(xs.shape[2]), fwd(bc.shape[2]), fwd(dt.shape[2]),
                  bwd(xs.shape[2]), bwd(bc.shape[2]), bwd(dt.shape[2]), _const_spec(alog.shape),
                  _const_spec(alog.T.shape),
                  _const_spec(tri.shape), _const_spec(expand.shape)],
        out_specs=[fwd(SSD_DIN), bwd(SSD_DIN)],
        out_shape=[jax.ShapeDtypeStruct((b, n, SSD_DIN), F32)] * 2,
        scratch_shapes=[pltpu.VMEM((2, SSD_NG, SSD_NS, SSD_DIN // SSD_NG), F32)],
        compiler_params=_cparams(("parallel", "arbitrary")), name="ssd_scan",
    )(xs, bc, dt, xs, bc, dt, alog, alog.T, tri, expand)


def _ret_kernel(qf_ref, kf_ref, vf_ref, qb_ref, kb_ref, vb_ref, logit_ref, yf_ref, yb_ref,
                st_ref, dec_ref, qin_ref, eout_ref):
    q = SCAN_CHUNK
    rows = lax.broadcasted_iota(jnp.int32, (q, q), 0)
    cols = lax.broadcasted_iota(jnp.int32, (q, q), 1)
    lo = cols < RET_DK
    lo_rows = rows < RET_DK
    lg_row = -_softplus(-logit_ref[...])
    npairs = RET_H // 2

    @pl.when(pl.program_id(1) == 0)
    def _():
        st_ref[...] = jnp.zeros_like(st_ref)
        dist = jnp.abs(rows - cols).astype(F32)
        for d in range(2):
            tri = (cols <= rows) if d == 0 else (cols >= rows)
            steps_in = (rows + 1 if d == 0 else q - rows).astype(F32)
            steps_out = (q - 1 - rows if d == 0 else rows).astype(F32)
            for hh in range(RET_H):
                lg = lg_row[:, RET_H * d + hh:RET_H * d + hh + 1]
                dec_ref[d, hh] = jnp.exp(jnp.where(tri, dist * lg, -jnp.inf))
                own = lo if hh % 2 == 0 else jnp.logical_not(lo)
                qin_ref[d, hh] = jnp.where(own, jnp.exp(steps_in * lg), 0.0).astype(BF16)
                eout_ref[d, hh] = jnp.exp(steps_out * lg).astype(BF16)

    for d, (q_ref, k_ref, v_ref, y_ref) in enumerate(((qf_ref, kf_ref, vf_ref, yf_ref),
                                                       (qb_ref, kb_ref, vb_ref, yb_ref))):
        qps = [q_ref[0, :, LANES * pr:LANES * (pr + 1)] for pr in range(npairs)]
        kts = [k_ref[0, :, LANES * pr:LANES * (pr + 1)].astype(F32).T for pr in range(npairs)]
        scs = [jnp.dot(qps[pr], jnp.concatenate([jnp.where(lo_rows, kts[pr], 0.0), jnp.where(lo_rows, 0.0, kts[pr])],
                                                axis=1).astype(BF16), preferred_element_type=F32)
               for pr in range(npairs)]
        news = []
        for pr in range(npairs):
            vdes = [v_ref[0, :, RET_DV * (2 * pr + e):RET_DV * (2 * pr + e + 1)] * eout_ref[d, 2 * pr + e]
                    for e in range(2)]
            news.append(jnp.dot(kts[pr].astype(BF16), jnp.concatenate(vdes, axis=1), preferred_element_type=F32))
        for pr in range(npairs):
            for e in range(2):
                hh = 2 * pr + e
                m = scs[pr][:, q * e:q * (e + 1)] * dec_ref[d, hh]
                lhs = jnp.concatenate([m.astype(BF16), qps[pr] * qin_ref[d, hh]], axis=1)
                st_prev = st_ref[d, hh]
                rhs = jnp.concatenate([v_ref[0, :, RET_DV * hh:RET_DV * (hh + 1)], st_prev.astype(BF16)], axis=0)
                y_ref[0, :, RET_DV * hh:RET_DV * (hh + 1)] = jnp.dot(lhs, rhs, preferred_element_type=F32)
                gam = jnp.exp(q * lg_row[:, RET_H * d + hh:RET_H * d + hh + 1])
                st_ref[d, hh] = st_prev * gam + news[pr][:, RET_DV * e:RET_DV * (e + 1)]


def _ret_scan(rq, rk, rv, logit, ctx_len):
    b, n, _ = rq.shape
    nc = n // SCAN_CHUNK
    cc = ctx_len // SCAN_CHUNK
    fwd = lambda w: pl.BlockSpec((1, SCAN_CHUNK, w), lambda i, s: (i, s, 0))
    bwd = lambda w: pl.BlockSpec((1, SCAN_CHUNK, w), lambda i, s: (i, _bwd_chunk(s, nc, cc), 0))
    return pl.pallas_call(
        _ret_kernel,
        grid=(b, nc),
        in_specs=[fwd(rq.shape[2]), fwd(rk.shape[2]), fwd(rv.shape[2]),
                  bwd(rq.shape[2]), bwd(rk.shape[2]), bwd(rv.shape[2]), _const_spec(logit.shape)],
        out_specs=[fwd(RET_H * RET_DV), bwd(RET_H * RET_DV)],
        out_shape=[jax.ShapeDtypeStruct((b, n, RET_H * RET_DV), F32)] * 2,
        scratch_shapes=[pltpu.VMEM((2, RET_H, LANES, RET_DV), F32), pltpu.VMEM((2, RET_H, SCAN_CHUNK, SCAN_CHUNK), F32),
                        pltpu.VMEM((2, RET_H, SCAN_CHUNK, LANES), BF16), pltpu.VMEM((2, RET_H, SCAN_CHUNK, RET_DV), BF16)],
        compiler_params=_cparams(("parallel", "arbitrary")), name="ret_scan",
    )(rq, rk, rv, rq, rk, rv, logit)


def _post_ssm_kernel(x_ref, yf_ref, yb_ref, xs_ref, z_ref, rf_ref, rb_ref, rg_ref, mod_ref, dsk_ref, sn_ref, rn_ref,
                     wo_ref, n2_ref, w13_ref, w2_ref, o_ref):
    z = z_ref[0]
    y = (yf_ref[0] + yb_ref[0] + dsk_ref[...] * xs_ref[0]) * (z * jax.nn.sigmoid(z))
    gw = SSD_DIN // SSD_NG
    o = None
    for g in range(SSD_NG):
        yg = (_rms(y[:, gw * g:gw * (g + 1)]) * sn_ref[:, gw * g:gw * (g + 1)]).astype(BF16)
        part = jnp.dot(yg, wo_ref[gw * g:gw * (g + 1), :], preferred_element_type=F32)
        o = part if o is None else o + part
    rg = rg_ref[0]
    gate = rg * jax.nn.sigmoid(rg)
    for hh in range(RET_H):
        sl = slice(RET_DV * hh, RET_DV * (hh + 1))
        yr = rf_ref[0, :, sl] + rb_ref[0, :, sl]
        mu = jnp.mean(yr, axis=-1, keepdims=True)
        dev = yr - mu
        var = jnp.mean(dev * dev, axis=-1, keepdims=True)
        yn = (dev * lax.rsqrt(var + EPS) * rn_ref[:, sl] * gate[:, sl]).astype(BF16)
        o = o + jnp.dot(yn, wo_ref[SSD_DIN + RET_DV * hh:SSD_DIN + RET_DV * (hh + 1), :], preferred_element_type=F32)
    x1 = x_ref[0] + mod_ref[0, 0, 2:3, :] * o
    o_ref[0] = _swiglu_tail(x1, mod_ref, n2_ref, w13_ref, w2_ref)


def _post_ssm(xj, yf, yb, xs, z, rf, rb, rg, mods, dsk, sn, rn, wo, n2, w13, w2, ctx_len):
    b, n, d = xj.shape
    off = ctx_len // ROW_TILE
    nt = n // ROW_TILE - off
    lat = lambda w: pl.BlockSpec((1, ROW_TILE, w), lambda i, t: (i, t + off, 0))
    return pl.pallas_call(
        _post_ssm_kernel,
        grid=(b, nt),
        in_specs=[lat(d)] * 8 + [pl.BlockSpec((1, 1, 6, d), lambda i, t: (i, 1, 0, 0)),
                                 _const_spec(dsk.shape), _const_spec(sn.shape), _const_spec(rn.shape),
                                 _const_spec(wo.shape), _const_spec(n2.shape), _const_spec(w13.shape),
                                 _const_spec(w2.shape)],
        out_specs=pl.BlockSpec((1, ROW_TILE, d), lambda i, t: (i, t, 0)),
        out_shape=jax.ShapeDtypeStruct((b, nt * ROW_TILE, d), F32),
        compiler_params=_cparams(("parallel", "arbitrary")), name="post_ssm",
    )(xj, yf, yb, xs, z, rf, rb, rg, mods, dsk, sn, rn, wo, n2, w13, w2)


def _ssm_layer_params(w_in, conv_w, conv_b, dt_bias, a_log, d_skip, ssd_norm, ret_logit, ret_norm, w_out):
    z, xbc, dt, rq, rk, rv, rg = jnp.split(
        w_in, np.cumsum([SSD_DIN, XBC, 2 * SSD_NH, RET_H * RET_DK, RET_H * RET_DK, RET_H * RET_DV]).tolist(), axis=1)
    dt = jnp.pad(dt, ((0, 0), (0, LANES - 2 * SSD_NH)))
    win = jnp.concatenate([z, xbc, rq, rk, rv, rg, dt], axis=1).astype(BF16)
    pad_row = lambda a: jnp.pad(a.reshape(-1), (0, LANES - a.size)).reshape(1, LANES)
    return dict(win=win, cw=conv_w, cb=conv_b.reshape(1, -1), dtb=pad_row(dt_bias), alog=pad_row(a_log),
                dsk=jnp.repeat(d_skip, SSD_P).reshape(1, -1), sn=ssd_norm.reshape(1, -1), logit=pad_row(ret_logit),
                rn=ret_norm.reshape(1, -1), wo=w_out.astype(BF16))


def _ssm_layer(xj, mods, prm, tabs, n1, n2, w13, w2, ctx_len):
    z, xs, bc, dt, rq, rk, rv, rg = _pre_ssm(xj, mods, n1, prm['win'], prm['cw'], prm['cb'], prm['dtb'],
                                             tabs[0], tabs[1], ctx_len)
    yf, yb = _ssd_scan(xs, bc, dt, prm['alog'], ctx_len)
    rf, rb = _ret_scan(rq, rk, rv, prm['logit'], ctx_len)
    return _post_ssm(xj, yf, yb, xs, z, rf, rb, rg, mods, prm['dsk'], prm['sn'], prm['rn'], prm['wo'], n2, w13, w2,
                     ctx_len)


def _rope_angles(seq_len, ctx_len, dim):
    rows = seq_len // GRID_W
    rr, cc = jnp.meshgrid(jnp.arange(rows, dtype=F32), jnp.arange(GRID_W, dtype=F32), indexing='ij')
    quarter = dim // 4
    inv = ROPE_THETA ** (-jnp.arange(quarter, dtype=F32) / quarter)
    ang = jnp.concatenate([rr.reshape(-1)[:, None] * inv, cc.reshape(-1)[:, None] * inv], axis=-1)
    cos = jnp.concatenate([jnp.ones((ctx_len, dim // 2), F32), jnp.cos(ang)], axis=0)
    sin = jnp.concatenate([jnp.zeros((ctx_len, dim // 2), F32), jnp.sin(ang)], axis=0)
    return cos, sin


def _interleaved_tables(cos, sin):
    c = jnp.repeat(cos, 2, axis=-1)
    s = jnp.stack([-sin, sin], axis=-1).reshape(sin.shape[0], -1)
    return c, s


def _attn_layer_params(w_in, gqa_qn, gqa_kn, mla_qa_norm, mla_wq_b, mla_kva_norm, mla_wkv_b, mla_qn, mla_kn, w_out):
    d = w_in.shape[0]
    qa, ka, va, ql, kvl, kr = jnp.split(w_in, np.cumsum([512, 128, 128, MLA_Q_RANK, MLA_KV_RANK]).tolist(), axis=1)
    qa = qa.reshape(d, GQA_HQ, HEAD_DIM)
    qa = jnp.concatenate([jnp.concatenate([qa[:, j], qa[:, j + 4]], axis=1) for j in range(4)], axis=1)
    kr = jnp.pad(kr, ((0, 0), (MLA_DNOPE, LANES - MLA_DQK)))
    win = jnp.concatenate([qa, ka, va, ql, kvl, kr], axis=1).astype(BF16)
    wqb = jnp.pad(mla_wq_b.reshape(MLA_Q_RANK, MLA_H, MLA_DQK), ((0, 0), (0, 0), (0, LANES - MLA_DQK)))
    wqb = wqb.reshape(MLA_Q_RANK, MLA_H * LANES).astype(BF16)
    wkv = mla_wkv_b.reshape(MLA_KV_RANK, MLA_H, MLA_DNOPE + HEAD_DIM)
    wk = jnp.pad(wkv[:, :, :MLA_DNOPE], ((0, 0), (0, 0), (0, LANES - MLA_DNOPE))).reshape(MLA_KV_RANK, MLA_H * LANES)
    wv = wkv[:, :, MLA_DNOPE:].reshape(MLA_KV_RANK, MLA_H * HEAD_DIM)
    pad_gain = lambda g: jnp.pad(g, (0, LANES - MLA_DQK)).reshape(1, LANES)
    woa = w_out[:512].reshape(GQA_HQ, HEAD_DIM, d)
    woa = jnp.concatenate([jnp.concatenate([woa[j], woa[j + 4]], axis=0) for j in range(4)], axis=0)
    wo = jnp.concatenate([woa, w_out[512:]], axis=0).astype(BF16)
    return dict(win=win, gq=jnp.tile(gqa_qn, 2).reshape(1, LANES), gk=jnp.tile(gqa_kn, 2).reshape(1, LANES),
                gqa=mla_qa_norm.reshape(1, -1), gkva=mla_kva_norm.reshape(1, -1), wqb=wqb, wk=wk.astype(BF16),
                wv=wv.astype(BF16), gmq=pad_gain(mla_qn), gmk=pad_gain(mla_kn), wo=wo)


def _attn_tables(seq_len, ctx_len):
    cg, sg = _interleaved_tables(*_rope_angles(seq_len, ctx_len, HEAD_DIM))
    cg, sg = jnp.tile(cg, (1, 2)), jnp.tile(sg, (1, 2))
    cm, sm = _interleaved_tables(*_rope_angles(seq_len, ctx_len, MLA_DROPE))
    n = cm.shape[0]
    cm = jnp.concatenate([jnp.ones((n, MLA_DNOPE), F32), cm, jnp.ones((n, LANES - MLA_DQK), F32)], axis=1)
    sm = jnp.pad(sm, ((0, 0), (MLA_DNOPE, LANES - MLA_DQK)))
    return cg, sg, cm, sm


def _layer_mods(mods_i, batch):
    d = mods_i.shape[1] // 6
    lat = mods_i[:batch].reshape(batch, 1, 6, d)
    ctx = jnp.broadcast_to(mods_i[batch].reshape(1, 1, 6, d), (batch, 1, 6, d))
    return jnp.concatenate([ctx, lat], axis=1)


def _attention_layer(xj, mods, prm, tabs, n1, n2, w13, w2, ctx_len):
    qg, kg, vglo, vghi, qm, km, vmlo, vmhi = _pre_attn(
        xj, mods, n1, prm['win'], prm['gq'], prm['gk'], prm['gqa'], prm['gkva'], prm['wqb'], prm['wk'], prm['wv'],
        prm['gmq'], prm['gmk'], *tabs)
    oa = _attention(qg, kg, vglo, vghi, shared_k=True, ctx_len=ctx_len)
    ob = _attention(qm, km, vmlo, vmhi, shared_k=False, ctx_len=ctx_len)
    return _post_attn(xj, oa, ob, mods, prm['wo'], n2, w13, w2)


def _all_mods(c, c_ctx, mod_w, mod_b):
    batch = c.shape[0]
    rows = -(-(batch + 1) // 8) * 8
    cvec = jnp.concatenate([c, c_ctx[None, :], jnp.zeros((rows - batch - 1, c.shape[1]), F32)], axis=0)
    mods = _modulation(cvec, mod_w, mod_b)
    return [_layer_mods(mods[i], batch) for i in range(mod_w.shape[0])]


def kernel(x, c, ctx, c_ctx, mod_w, mod_b, norm1_w, norm2_w, ffn_w13, ffn_w2, attn_w_in, gqa_qn, gqa_kn, mla_qa_norm, mla_wq_b, mla_kva_norm, mla_wkv_b, mla_qn, mla_kn, attn_w_out, ssm_w_in, ssd_conv_w, ssd_conv_b, ssd_dt_bias, ssd_a_log, ssd_d, ssd_norm, ret_decay_logit, ret_norm, ssm_w_out):
    ctx_len, seq_len = ctx.shape[1], x.shape[1]
    assert ctx_len % ROW_TILE == 0 and seq_len % ROW_TILE == 0 and seq_len % GRID_W == 0
    assert mod_w.shape[0] == 2, "one attention layer followed by one SSM layer"
    mods = _all_mods(c, c_ctx, mod_w, mod_b)
    tabs = _attn_tables(seq_len, ctx_len)
    xj = jnp.concatenate([ctx, x], axis=1)
    prm0 = _attn_layer_params(attn_w_in[0], gqa_qn[0], gqa_kn[0], mla_qa_norm[0], mla_wq_b[0], mla_kva_norm[0],
                              mla_wkv_b[0], mla_qn[0], mla_kn[0], attn_w_out[0])
    xj = _attention_layer(xj, mods[0], prm0, tabs, norm1_w[0].reshape(1, -1), norm2_w[0].reshape(1, -1),
                          ffn_w13[0].astype(BF16), ffn_w2[0].astype(BF16), ctx_len)
    prm1 = _ssm_layer_params(ssm_w_in[0], ssd_conv_w[0], ssd_conv_b[0], ssd_dt_bias[0], ssd_a_log[0], ssd_d[0],
                             ssd_norm[0], ret_decay_logit[0], ret_norm[0], ssm_w_out[0])
    return _ssm_layer(xj, mods[1], prm1, tabs, norm1_w[1].reshape(1, -1), norm2_w[1].reshape(1, -1),
                      ffn_w13[1].astype(BF16), ffn_w2[1].astype(BF16), ctx_len)


def _debug_layer1(x, c, ctx, c_ctx, mod_w, mod_b, norm1_w, norm2_w, ffn_w13, ffn_w2, ssm_w_in, ssd_conv_w, ssd_conv_b,
                  ssd_dt_bias, ssd_a_log, ssd_d, ssd_norm, ret_decay_logit, ret_norm, ssm_w_out, **_):
    ctx_len, seq_len = ctx.shape[1], x.shape[1]
    mods = _all_mods(c, c_ctx, mod_w, mod_b)
    tabs = _attn_tables(seq_len, ctx_len)
    xj = jnp.concatenate([ctx, x], axis=1)
    prm1 = _ssm_layer_params(ssm_w_in[0], ssd_conv_w[0], ssd_conv_b[0], ssd_dt_bias[0], ssd_a_log[0], ssd_d[0],
                             ssd_norm[0], ret_decay_logit[0], ret_norm[0], ssm_w_out[0])
    return _ssm_layer(xj, mods[1], prm1, tabs, norm1_w[1].reshape(1, -1), norm2_w[1].reshape(1, -1),
                      ffn_w13[1].astype(BF16), ffn_w2[1].astype(BF16), ctx_len)


def _debug_layer0(x, c, ctx, c_ctx, mod_w, mod_b, norm1_w, norm2_w, ffn_w13, ffn_w2, attn_w_in, gqa_qn, gqa_kn,
                  mla_qa_norm, mla_wq_b, mla_kva_norm, mla_wkv_b, mla_qn, mla_kn, attn_w_out, **_):
    ctx_len, seq_len = ctx.shape[1], x.shape[1]
    mods = _all_mods(c, c_ctx, mod_w, mod_b)
    xj = jnp.concatenate([ctx, x], axis=1)
    prm = _attn_layer_params(attn_w_in[0], gqa_qn[0], gqa_kn[0], mla_qa_norm[0], mla_wq_b[0], mla_kva_norm[0],
                             mla_wkv_b[0], mla_qn[0], mla_kn[0], attn_w_out[0])
    return _attention_layer(xj, mods[0], prm, _attn_tables(seq_len, ctx_len), norm1_w[0].reshape(1, -1),
                            norm2_w[0].reshape(1, -1), ffn_w13[0].astype(BF16), ffn_w2[0].astype(BF16), ctx_len)
```

```python
import functools
import math

import jax
import jax.numpy as jnp
import numpy as np
from jax import lax
from jax.experimental import pallas as pl
from jax.experimental.pallas import tpu as pltpu

F32 = jnp.float32
BF16 = jnp.bfloat16

EPS = 1e-6
ROPE_THETA = 10000.0
GRID_W = 64
HEAD_DIM = 64
LANES = 128
ROW_TILE = 256
SCAN_CHUNK = 128
Q_TILE = 512
VMEM_LIMIT = 56 * 1024 * 1024

GQA_HQ = 8
GQA_HKV = 2
MLA_H = 8
MLA_DNOPE = 64
MLA_DROPE = 32
MLA_DQK = MLA_DNOPE + MLA_DROPE
MLA_Q_RANK = 384
MLA_KV_RANK = 256
LOG2E = 1.4426950408889634


def _cparams(sem):
    return pltpu.CompilerParams(dimension_semantics=sem, vmem_limit_bytes=VMEM_LIMIT)


def _const_spec(shape):
    nd = len(shape)
    return pl.BlockSpec(shape, lambda *_: (0,) * nd, pipeline_mode=pl.Buffered(1))


def _rms(x):
    return x * lax.rsqrt(jnp.mean(x * x, axis=-1, keepdims=True) + EPS)


def _swap_pairs(y):
    lane = lax.broadcasted_iota(jnp.int32, y.shape, y.ndim - 1)
    nxt = pltpu.roll(y, LANES - 1, y.ndim - 1)
    prv = pltpu.roll(y, 1, y.ndim - 1)
    return jnp.where(lane % 2 == 0, nxt, prv)


def _rope(y, cos, sin_signed):
    return y * cos + _swap_pairs(y) * sin_signed


def _mod_kernel(c_ref, w_ref, b_ref, o_ref):
    c = c_ref[...]
    a = (c * jax.nn.sigmoid(c)).astype(BF16)
    o_ref[0] = jnp.dot(a, w_ref[0].astype(BF16), preferred_element_type=F32) + b_ref[0]


def _modulation(cvec, mod_w, mod_b):
    depth, d, n = mod_w.shape
    rows = cvec.shape[0]
    tn = 512
    return pl.pallas_call(
        _mod_kernel,
        grid=(depth, n // tn),
        in_specs=[pl.BlockSpec((rows, d), lambda i, j: (0, 0)),
                  pl.BlockSpec((1, d, tn), lambda i, j: (i, 0, j)),
                  pl.BlockSpec((1, 1, tn), lambda i, j: (i, 0, j))],
        out_specs=pl.BlockSpec((1, rows, tn), lambda i, j: (i, 0, j)),
        out_shape=jax.ShapeDtypeStruct((depth, rows, n), F32),
        compiler_params=_cparams(("arbitrary", "arbitrary")), name="modulation",
    )(cvec, mod_w, mod_b.reshape(depth, 1, n))


def _pre_attn_kernel(ctx_ref, x_ref, mod_ref, n1_ref, win_ref, gq_ref, gk_ref, gqa_ref, gkva_ref, wqb_ref, wk_ref,
                     wv_ref, gmq_ref, gmk_ref, cg_ref, sg_ref, cm_ref, sm_ref,
                     qg_ref, kg_ref, vglo_ref, vghi_ref, qm_ref, km_ref, vmlo_ref, vmhi_ref, *, ctx_tiles):
    x = jnp.where(pl.program_id(1) < ctx_tiles, ctx_ref[0], x_ref[0])
    shift = mod_ref[0, 0, 0:1, :]
    scale = mod_ref[0, 0, 1:2, :]
    u = _rms(x) * n1_ref[...] * (1.0 + scale) + shift
    h = jnp.dot(u.astype(BF16), win_ref[...], preferred_element_type=F32)

    tm = x.shape[0]
    lane = lax.broadcasted_iota(jnp.int32, (tm, LANES), 1)
    lo = lane < HEAD_DIM
    cg, sg, cm, sm = cg_ref[...], sg_ref[...], cm_ref[...], sm_ref[...]

    def pair_norm_rope(blk, gain):
        sq = blk * blk
        ss_lo = jnp.sum(jnp.where(lo, sq, 0.0), axis=-1, keepdims=True)
        ss_hi = jnp.sum(jnp.where(lo, 0.0, sq), axis=-1, keepdims=True)
        r = lax.rsqrt(jnp.where(lo, ss_lo, ss_hi) * (1.0 / HEAD_DIM) + EPS)
        return _rope(blk * r * gain, cg, sg)

    def head_norm_rope(blk, gain):
        r = lax.rsqrt(jnp.sum(blk * blk, axis=-1, keepdims=True) * (1.0 / MLA_DQK) + EPS)
        return _rope(blk * r * gain, cm, sm)

    qscale = (HEAD_DIM ** -0.5) * LOG2E
    for j in range(4):
        y = pair_norm_rope(h[:, LANES * j:LANES * (j + 1)], gq_ref[...]) * qscale
        qg_ref[0, :, LANES * (2 * j):LANES * (2 * j + 1)] = jnp.where(lo, y, 0.0).astype(BF16)
        qg_ref[0, :, LANES * (2 * j + 1):LANES * (2 * j + 2)] = jnp.where(lo, 0.0, y).astype(BF16)
    kg_ref[0] = pair_norm_rope(h[:, 512:640], gk_ref[...]).astype(BF16)

    def put_values(blk, lo_ref, hi_ref, sl):
        lo_ref[0, :, sl] = jnp.where(lo, blk, jnp.where(lane == HEAD_DIM, 1.0, 0.0)).astype(BF16)
        hi_ref[0, :, sl] = jnp.where(lo, jnp.where(lane == 0, 1.0, 0.0), blk).astype(BF16)

    put_values(h[:, 640:768], vglo_ref, vghi_ref, slice(0, LANES))

    ql = (_rms(h[:, 768:1152]) * gqa_ref[...]).astype(BF16)
    qm = jnp.dot(ql, wqb_ref[...], preferred_element_type=F32)
    kvl = (_rms(h[:, 1152:1408]) * gkva_ref[...]).astype(BF16)
    kn = jnp.dot(kvl, wk_ref[...], preferred_element_type=F32)
    vm = jnp.dot(kvl, wv_ref[...], preferred_element_type=F32)
    for j in range(MLA_H // 2):
        put_values(vm[:, LANES * j:LANES * (j + 1)], vmlo_ref, vmhi_ref, slice(LANES * j, LANES * (j + 1)))
    kr = h[:, 1408:1536]
    mscale = (MLA_DQK ** -0.5) * LOG2E
    for hh in range(MLA_H):
        sl = slice(LANES * hh, LANES * (hh + 1))
        qm_ref[0, :, sl] = (head_norm_rope(qm[:, sl], gmq_ref[...]) * mscale).astype(BF16)
        km_ref[0, :, sl] = head_norm_rope(kn[:, sl] + kr, gmk_ref[...]).astype(BF16)


def _ctx_lat_specs(ct, d):
    return [pl.BlockSpec((1, ROW_TILE, d), lambda i, t: (i, jnp.minimum(t, ct - 1), 0)),
            pl.BlockSpec((1, ROW_TILE, d), lambda i, t: (i, jnp.maximum(t - ct, 0), 0))]


def _pre_attn(ctx, x, mods, n1, win, gq, gk, gqa, gkva, wqb, wk, wv, gmq, gmk, cg, sg, cm, sm):
    b, _, d = x.shape
    ct = ctx.shape[1] // ROW_TILE
    n = ctx.shape[1] + x.shape[1]
    nt = n // ROW_TILE
    row = lambda w: pl.BlockSpec((1, ROW_TILE, w), lambda i, t: (i, t, 0))
    qrow = lambda w: pl.BlockSpec((1, ROW_TILE, w), lambda i, t: (i, (t + nt - ct) % nt, 0))
    tab = pl.BlockSpec((ROW_TILE, LANES), lambda i, t: (t, 0))
    outs = [(8 * LANES, BF16), (LANES, BF16), (LANES, BF16), (LANES, BF16),
            (8 * LANES, BF16), (8 * LANES, BF16), (4 * LANES, BF16), (4 * LANES, BF16)]
    return pl.pallas_call(
        functools.partial(_pre_attn_kernel, ctx_tiles=ct),
        grid=(b, nt),
        in_specs=_ctx_lat_specs(ct, d) + [
                  pl.BlockSpec((1, 1, 6, d), lambda i, t: (i, jnp.minimum(t, 1), 0, 0)),
                  _const_spec(n1.shape), _const_spec(win.shape), _const_spec(gq.shape), _const_spec(gk.shape),
                  _const_spec(gqa.shape), _const_spec(gkva.shape), _const_spec(wqb.shape), _const_spec(wk.shape),
                  _const_spec(wv.shape), _const_spec(gmq.shape), _const_spec(gmk.shape), tab, tab, tab, tab],
        out_specs=[(qrow if i in (0, 4) else row)(w) for i, (w, _) in enumerate(outs)],
        out_shape=[jax.ShapeDtypeStruct((b, n, w), dt) for w, dt in outs],
        compiler_params=_cparams(("parallel", "arbitrary")), name="pre_attn",
    )(ctx, x, mods, n1, win, gq, gk, gqa, gkva, wqb, wk, wv, gmq, gmk, cg, sg, cm, sm)


def _attn_kernel(q_ref, k_ref, vlo_ref, vhi_ref, o_ref, *, shared_k, ctx_len):
    t = pl.program_id(2)
    last = pl.num_programs(2) - 1

    def run(nq, nk):
        s, acc = [], []
        for idx in range(2):
            q = q_ref[0, :nq, LANES * idx:LANES * (idx + 1)]
            koff = 0 if shared_k else LANES * idx
            s.append(lax.dot_general(q, k_ref[0, :nk, koff:koff + LANES], (((1,), (1,)), ((), ())),
                                     preferred_element_type=F32))
        for idx, v_ref in enumerate((vlo_ref, vhi_ref)):
            p = jnp.exp2(s[idx] - jnp.max(s[idx], axis=-1, keepdims=True)).astype(BF16)
            acc.append(jnp.dot(p, v_ref[0, :nk, :], preferred_element_type=F32))
        out0 = acc[0] / acc[0][:, HEAD_DIM:HEAD_DIM + 1]
        out1 = acc[1] / acc[1][:, 0:1]
        lane = lax.broadcasted_iota(jnp.int32, out0.shape, 1)
        o_ref[0, :nq, :] = jnp.where(lane < HEAD_DIM, out0, out1).astype(o_ref.dtype)

    @pl.when(t < last)
    def _():
        run(q_ref.shape[1], k_ref.shape[1])

    @pl.when(t == last)
    def _():
        run(ctx_len, ctx_len)
        if ctx_len < q_ref.shape[1]:
            o_ref[0, ctx_len:, :] = jnp.zeros((q_ref.shape[1] - ctx_len, o_ref.shape[2]), o_ref.dtype)


def _attention(q, k, vlo, vhi, *, shared_k, ctx_len):
    b, n, _ = q.shape
    assert (n - ctx_len) % Q_TILE == 0 and ctx_len <= Q_TILE
    nt = (n - ctx_len) // Q_TILE + 1
    pairs = q.shape[2] // (2 * LANES)
    kw = LANES if shared_k else 2 * LANES
    kmap = (lambda i, j, t: (i, 0, 0)) if shared_k else (lambda i, j, t: (i, 0, j))
    return pl.pallas_call(
        functools.partial(_attn_kernel, shared_k=shared_k, ctx_len=ctx_len),
        grid=(b, pairs, nt),
        in_specs=[pl.BlockSpec((1, Q_TILE, 2 * LANES), lambda i, j, t: (i, t, j)),
                  pl.BlockSpec((1, n, kw), kmap),
                  pl.BlockSpec((1, n, LANES), kmap),
                  pl.BlockSpec((1, n, LANES), kmap)],
        out_specs=pl.BlockSpec((1, Q_TILE, LANES), lambda i, j, t: (i, t, j)),
        out_shape=jax.ShapeDtypeStruct((b, n, pairs * LANES), BF16),
        compiler_params=_cparams(("parallel", "parallel", "arbitrary")),
        name="attn_gqa" if shared_k else "attn_mla",
    )(q, k, vlo, vhi)


def _swiglu_tail(x1, mod_ref, n2_ref, w13_ref, w2_ref):
    shift, scale, gate = mod_ref[0, 0, 3:4, :], mod_ref[0, 0, 4:5, :], mod_ref[0, 0, 5:6, :]
    f_in = _rms(x1) * n2_ref[...] * (1.0 + scale) + shift
    h = jnp.dot(f_in.astype(BF16), w13_ref[...], preferred_element_type=F32)
    hid = h.shape[1] // 2
    gte, up = h[:, :hid], h[:, hid:]
    a = (gte * jax.nn.sigmoid(gte) * up).astype(BF16)
    return x1 + gate * jnp.dot(a, w2_ref[...], preferred_element_type=F32)


def _post_attn_kernel(ctx_ref, x_ref, oa_ref, ob_ref, mod_ref, wo_ref, n2_ref, w13_ref, w2_ref, y_ref, *, ctx_tiles):
    half = oa_ref.shape[2]
    o = jnp.dot(oa_ref[0], wo_ref[:half, :], preferred_element_type=F32)
    o = o + jnp.dot(ob_ref[0], wo_ref[half:, :], preferred_element_type=F32)
    x = jnp.where(pl.program_id(1) < ctx_tiles, ctx_ref[0], x_ref[0])
    x1 = x + mod_ref[0, 0, 2:3, :] * o
    y_ref[0] = _swiglu_tail(x1, mod_ref, n2_ref, w13_ref, w2_ref)


def _post_attn(ctx, x, oa, ob, mods, wo, n2, w13, w2):
    b, _, d = x.shape
    ct = ctx.shape[1] // ROW_TILE
    n = ctx.shape[1] + x.shape[1]
    nt = n // ROW_TILE
    row = lambda w: pl.BlockSpec((1, ROW_TILE, w), lambda i, t: (i, t, 0))
    qrow = lambda w: pl.BlockSpec((1, ROW_TILE, w), lambda i, t: (i, (t + nt - ct) % nt, 0))
    return pl.pallas_call(
        functools.partial(_post_attn_kernel, ctx_tiles=ct),
        grid=(b, nt),
        in_specs=_ctx_lat_specs(ct, d) + [
                  qrow(oa.shape[2]), qrow(ob.shape[2]),
                  pl.BlockSpec((1, 1, 6, d), lambda i, t: (i, jnp.minimum(t, 1), 0, 0)),
                  _const_spec(wo.shape), _const_spec(n2.shape), _const_spec(w13.shape), _const_spec(w2.shape)],
        out_specs=row(d),
        out_shape=jax.ShapeDtypeStruct((b, n, d), F32),
        compiler_params=_cparams(("parallel", "arbitrary")), name="post_attn",
    )(ctx, x, oa, ob, mods, wo, n2, w13, w2)


HALO = 8
SSD_DIN = 1024
SSD_NH = 16
SSD_NG = 2
SSD_NS = 128
SSD_P = 64
SSD_CONV = 5
XBC = SSD_DIN + 2 * SSD_NG * SSD_NS
RET_H = 8
RET_DK = 64
RET_DV = 128
C_Z, C_XBC, C_RQ, C_RK, C_RV, C_RG, C_DT, C_END = 0, 1024, 2560, 3072, 3584, 4608, 5632, 5760


def _softplus(x):
    return jnp.maximum(x, 0.0) + jnp.log1p(jnp.exp(-jnp.abs(x)))


def _pre_ssm_kernel(xp_ref, x_ref, xn_ref, mod_ref, n1_ref, win_ref, cw_ref, cb_ref, dtb_ref, cg_ref, sg_ref,
                    z_ref, xs_ref, bc_ref, dt_ref, rq_ref, rk_ref, rv_ref, rg_ref, ext_ref, *, ctx_tiles):
    t = pl.program_id(1)
    nt = pl.num_programs(1)
    tm = x_ref.shape[1]
    x_ext = jnp.concatenate([xp_ref[0], x_ref[0], xn_ref[0]], axis=0)
    u = _rms(x_ext) * n1_ref[...] * (1.0 + mod_ref[0, 0, 1:2, :]) + mod_ref[0, 0, 0:1, :]
    h = jnp.dot(u.astype(BF16), win_ref[...], preferred_element_type=F32)
    hm = h[HALO:HALO + tm]
    z_ref[0] = hm[:, C_Z:C_XBC]
    rg_ref[0] = hm[:, C_RG:C_DT]
    rv_ref[0] = hm[:, C_RV:C_RG].astype(BF16)
    dt_ref[0] = _softplus(hm[:, C_DT:C_END] + dtb_ref[...])

    rows = lax.broadcasted_iota(jnp.int32, (tm + 2 * HALO, 1), 0)
    has_prev = jnp.logical_and(t != 0, t != ctx_tiles)
    has_next = jnp.logical_and(t != ctx_tiles - 1, t != nt - 1)
    keep = jnp.logical_and(jnp.logical_or(rows >= HALO, has_prev), jnp.logical_or(rows < HALO + tm, has_next))
    ext_ref[...] = jnp.where(keep, h[:, C_XBC:C_RQ], 0.0)
    acc = cb_ref[...] + cw_ref[0:1, :] * ext_ref[pl.ds(HALO - 2, tm), :]
    for k in range(1, SSD_CONV):
        acc = acc + cw_ref[k:k + 1, :] * ext_ref[pl.ds(HALO - 2 + k, tm), :]
    act = acc * jax.nn.sigmoid(acc)
    xs_ref[0] = act[:, :SSD_DIN]
    bc_ref[0] = act[:, SSD_DIN:].astype(BF16)

    cg, sg = cg_ref[...], sg_ref[...]
    for j in range(RET_H * RET_DK // LANES):
        sl = slice(LANES * j, LANES * (j + 1))
        rq_ref[0, :, sl] = _rope(hm[:, C_RQ + LANES * j:C_RQ + LANES * (j + 1)], cg, sg).astype(BF16)
        rk = _rope(hm[:, C_RK + LANES * j:C_RK + LANES * (j + 1)], cg, sg) * (RET_DK ** -0.5)
        rk_ref[0, :, sl] = rk.astype(BF16)


def _pre_ssm(xj, mods, n1, win, cw, cb, dtb, cg, sg, ctx_len):
    b, n, d = xj.shape
    nt = n // ROW_TILE
    per = ROW_TILE // HALO
    row = lambda w: pl.BlockSpec((1, ROW_TILE, w), lambda i, t: (i, t, 0))
    tab = pl.BlockSpec((ROW_TILE, LANES), lambda i, t: (t, 0))
    outs = [(SSD_DIN, F32), (SSD_DIN, F32), (2 * SSD_NG * SSD_NS, BF16), (LANES, F32), (RET_H * RET_DK, BF16),
            (RET_H * RET_DK, BF16), (RET_H * RET_DV, BF16), (RET_H * RET_DV, F32)]
    return pl.pallas_call(
        functools.partial(_pre_ssm_kernel, ctx_tiles=ctx_len // ROW_TILE),
        grid=(b, nt),
        in_specs=[pl.BlockSpec((1, HALO, d), lambda i, t: (i, jnp.maximum(t * per - 1, 0), 0)),
                  row(d),
                  pl.BlockSpec((1, HALO, d), lambda i, t: (i, jnp.minimum((t + 1) * per, n // HALO - 1), 0)),
                  pl.BlockSpec((1, 1, 6, d), lambda i, t: (i, jnp.minimum(t, 1), 0, 0)),
                  _const_spec(n1.shape), _const_spec(win.shape), _const_spec(cw.shape), _const_spec(cb.shape),
                  _const_spec(dtb.shape), tab, tab],
        out_specs=[row(w) for w, _ in outs],
        out_shape=[jax.ShapeDtypeStruct((b, n, w), dt) for w, dt in outs],
        scratch_shapes=[pltpu.VMEM((ROW_TILE + 2 * HALO, XBC), F32)],
        compiler_params=_cparams(("parallel", "arbitrary")), name="pre_ssm",
    )(xj, xj, xj, mods, n1, win, cw, cb, dtb, cg, sg)


def _split_dot_rhs(m, x, parts):
    acc = None
    for _ in range(parts):
        p = x.astype(BF16)
        d = jnp.dot(m, p, preferred_element_type=F32)
        acc = d if acc is None else acc + d
        x = x - p.astype(F32)
    return acc


def _split_dot_lhs(x, m, parts):
    acc = None
    for _ in range(parts):
        p = x.astype(BF16)
        d = jnp.dot(p, m, preferred_element_type=F32)
        acc = d if acc is None else acc + d
        x = x - p.astype(F32)
    return acc


def _running_sum_rows(x, reverse):
    n = x.shape[0]
    row = lax.broadcasted_iota(jnp.int32, x.shape, 0)
    k = 1
    while k < n:
        if reverse:
            x = x + jnp.where(row < n - k, pltpu.roll(x, n - k, 0), 0.0)
        else:
            x = x + jnp.where(row >= k, pltpu.roll(x, k, 0), 0.0)
        k *= 2
    return x


def _bwd_chunk(s, nchunks, ctx_chunks):
    return jnp.where(s < ctx_chunks, ctx_chunks - 1 - s, nchunks - 1 - (s - ctx_chunks))


def _scan_masks():
    i = np.arange(SCAN_CHUNK)
    tri = np.stack([i[None, :] <= i[:, None], i[None, :] >= i[:, None]])
    lane, col = np.arange(LANES)[:, None], np.arange(SSD_DIN)[None, :] // SSD_P
    expand = np.stack([col + SSD_NH * d == lane for d in range(2)])
    return jnp.asarray(tri, BF16), jnp.asarray(expand, BF16)


def _ssd_kernel(xsf_ref, bcf_ref, dtf_ref, xsb_ref, bcb_ref, dtb_ref, alog_ref, alogc_ref, tri_ref, exp_ref,
                yf_ref, yb_ref, st_ref):
    @pl.when(pl.program_id(1) == 0)
    def _():
        st_ref[...] = jnp.zeros_like(st_ref)

    q = SCAN_CHUNK
    rows = lax.broadcasted_iota(jnp.int32, (q, q), 0)
    cols = lax.broadcasted_iota(jnp.int32, (q, q), 1)
    lo = cols < SSD_P
    a_row = -jnp.exp(alog_ref[...])
    a_col = -jnp.exp(alogc_ref[...])
    gw = SSD_DIN // SSD_NG

    dirs = ((xsf_ref, bcf_ref, dtf_ref, yf_ref), (xsb_ref, bcb_ref, dtb_ref, yb_ref))
    tris = (cols <= rows, cols >= rows)
    cs, c_ts, vdes, xsbs, cdxs = [], [], [], [], []
    for d, (xs_ref, _, dt_ref, _) in enumerate(dirs):
        xs = xs_ref[0]
        dt = dt_ref[0]
        la = dt * a_row
        c = _running_sum_rows(la, reverse=(d == 1))
        ctot = jnp.sum(la, axis=0, keepdims=True)
        expand = exp_ref[d]
        stacked = jnp.concatenate([dt * jnp.exp(ctot - c), jnp.broadcast_to(jnp.exp(ctot), (8, LANES))], axis=0)
        ex = _split_dot_lhs(stacked, expand, 2)
        vdes.append((xs * ex[0:q]).astype(BF16))
        cdxs.append(ex[q:q + 1])
        xsbs.append(xs.astype(BF16))
        c2 = c * LOG2E
        cs.append(c2)
        c_ts.append((c2 - jnp.log2(dt)).T)

    scs, cgs, rhss = {}, {}, {}
    for g in range(SSD_NG):
        for d, (_, bc_ref, _, _) in enumerate(dirs):
            bg = bc_ref[0, :, SSD_NS * g:SSD_NS * (g + 1)]
            cgp = bc_ref[0, :, SSD_NG * SSD_NS + SSD_NS * g:SSD_NG * SSD_NS + SSD_NS * (g + 1)]
            scs[d, g] = lax.dot_general(cgp, bg, (((1,), (1,)), ((), ())), preferred_element_type=F32)
            cgs[d, g] = cgp
            st_prev = st_ref[d, g]
            st_ref[d, g] = st_prev * cdxs[d][:, gw * g:gw * (g + 1)] + jnp.dot(
                bg.astype(F32).T.astype(BF16), vdes[d][:, gw * g:gw * (g + 1)], preferred_element_type=F32)
            st_b = st_prev.astype(BF16)
            for pr in range(gw // LANES):
                rhss[d, g, pr] = jnp.concatenate([xsbs[d][:, gw * g + LANES * pr:gw * g + LANES * (pr + 1)],
                                                  st_b[:, LANES * pr:LANES * (pr + 1)]], axis=0)

    for g in range(SSD_NG):
        for pr in range(gw // LANES):
            for d, (_, _, _, y_ref) in enumerate(dirs):
                outs = []
                for e in range(2):
                    ln = SSD_NH * d + (SSD_NH // SSD_NG) * g + 2 * pr + e
                    c_i = jnp.broadcast_to(cs[d][:, ln:ln + 1], (q, q))
                    m = scs[d, g] * jnp.exp2(jnp.where(tris[d], c_i - c_ts[d][ln:ln + 1, :], -jnp.inf))
                    lhs = jnp.concatenate([m.astype(BF16), cgs[d, g] * jnp.exp2(c_i).astype(BF16)], axis=1)
                    outs.append(jnp.dot(lhs, rhss[d, g, pr], preferred_element_type=F32))
                sl = slice(gw * g + LANES * pr, gw * g + LANES * (pr + 1))
                y_ref[0, :, sl] = jnp.where(lo, outs[0], outs[1])


def _ssd_scan(xs, bc, dt, alog, ctx_len):
    b, n, _ = xs.shape
    nc = n // SCAN_CHUNK
    cc = ctx_len // SCAN_CHUNK
    fwd = lambda w: pl.BlockSpec((1, SCAN_CHUNK, w), lambda i, s: (i, s, 0))
    bwd = lambda w: pl.BlockSpec((1, SCAN_CHUNK, w), lambda i, s: (i, _bwd_chunk(s, nc, cc), 0))
    tri, expand = _scan_masks()
    return pl.pallas_call(
        _ssd_kernel,
        grid=(b, nc),
        in_specs=[fwd(xs.shape[2]), fwd(bc.shape[2]), fwd(dt.shape[2]),
                  bwd(xs.shape[2]), bwd(bc.shape[2]), bwd(dt.shape[2]), _const_spec(alog.shape),
                  _const_spec(alog.T.shape),
                  _const_spec(tri.shape), _const_spec(expand.shape)],
        out_specs=[fwd(SSD_DIN), bwd(SSD_DIN)],
        out_shape=[jax.ShapeDtypeStruct((b, n, SSD_DIN), F32)] * 2,
        scratch_shapes=[pltpu.VMEM((2, SSD_NG, SSD_NS, SSD_DIN // SSD_NG), F32)],
        compiler_params=_cparams(("parallel", "arbitrary")), name="ssd_scan",
    )(xs, bc, dt, xs, bc, dt, alog, alog.T, tri, expand)


def _ret_kernel(qf_ref, kf_ref, vf_ref, qb_ref, kb_ref, vb_ref, logit_ref, yf_ref, yb_ref,
                st_ref, dec_ref, qin_ref, eout_ref):
    q = SCAN_CHUNK
    rows = lax.broadcasted_iota(jnp.int32, (q, q), 0)
    cols = lax.broadcasted_iota(jnp.int32, (q, q), 1)
    lo = cols < RET_DK
    lo_rows = rows < RET_DK
    lg_row = -_softplus(-logit_ref[...])
    npairs = RET_H // 2

    @pl.when(pl.program_id(1) == 0)
    def _():
        st_ref[...] = jnp.zeros_like(st_ref)
        dist = jnp.abs(rows - cols).astype(F32)
        for d in range(2):
            tri = (cols <= rows) if d == 0 else (cols >= rows)
            steps_in = (rows + 1 if d == 0 else q - rows).astype(F32)
            steps_out = (q - 1 - rows if d == 0 else rows).astype(F32)
            for hh in range(RET_H):
                lg = lg_row[:, RET_H * d + hh:RET_H * d + hh + 1]
                dec_ref[d, hh] = jnp.exp(jnp.where(tri, dist * lg, -jnp.inf))
                own = lo if hh % 2 == 0 else jnp.logical_not(lo)
                qin_ref[d, hh] = jnp.where(own, jnp.exp(steps_in * lg), 0.0).astype(BF16)
                eout_ref[d, hh] = jnp.exp(steps_out * lg).astype(BF16)

    for d, (q_ref, k_ref, v_ref, y_ref) in enumerate(((qf_ref, kf_ref, vf_ref, yf_ref),
                                                       (qb_ref, kb_ref, vb_ref, yb_ref))):
        qps = [q_ref[0, :, LANES * pr:LANES * (pr + 1)] for pr in range(npairs)]
        kts = [k_ref[0, :, LANES * pr:LANES * (pr + 1)].astype(F32).T for pr in range(npairs)]
        scs = [jnp.dot(qps[pr], jnp.concatenate([jnp.where(lo_rows, kts[pr], 0.0), jnp.where(lo_rows, 0.0, kts[pr])],
                                                axis=1).astype(BF16), preferred_element_type=F32)
               for pr in range(npairs)]
        news = []
        for pr in range(npairs):
            vdes = [v_ref[0, :, RET_DV * (2 * pr + e):RET_DV * (2 * pr + e + 1)] * eout_ref[d, 2 * pr + e]
                    for e in range(2)]
            news.append(jnp.dot(kts[pr].astype(BF16), jnp.concatenate(vdes, axis=1), preferred_element_type=F32))
        for pr in range(npairs):
            for e in range(2):
                hh = 2 * pr + e
                m = scs[pr][:, q * e:q * (e + 1)] * dec_ref[d, hh]
                lhs = jnp.concatenate([m.astype(BF16), qps[pr] * qin_ref[d, hh]], axis=1)
                st_prev = st_ref[d, hh]
                rhs = jnp.concatenate([v_ref[0, :, RET_DV * hh:RET_DV * (hh + 1)], st_prev.astype(BF16)], axis=0)
                y_ref[0, :, RET_DV * hh:RET_DV * (hh + 1)] = jnp.dot(lhs, rhs, preferred_element_type=F32)
                gam = jnp.exp(q * lg_row[:, RET_H * d + hh:RET_H * d + hh + 1])
                st_ref[d, hh] = st_prev * gam + news[pr][:, RET_DV * e:RET_DV * (e + 1)]


def _ret_scan(rq, rk, rv, logit, ctx_len):
    b, n, _ = rq.shape
    nc = n // SCAN_CHUNK
    cc = ctx_len // SCAN_CHUNK
    fwd = lambda w: pl.BlockSpec((1, SCAN_CHUNK, w), lambda i, s: (i, s, 0))
    bwd = lambda w: pl.BlockSpec((1, SCAN_CHUNK, w), lambda i, s: (i, _bwd_chunk(s, nc, cc), 0))
    return pl.pallas_call(
        _ret_kernel,
        grid=(b, nc),
        in_specs=[fwd(rq.shape[2]), fwd(rk.shape[2]), fwd(rv.shape[2]),
                  bwd(rq.shape[2]), bwd(rk.shape[2]), bwd(rv.shape[2]), _const_spec(logit.shape)],
        out_specs=[fwd(RET_H * RET_DV), bwd(RET_H * RET_DV)],
        out_shape=[jax.ShapeDtypeStruct((b, n, RET_H * RET_DV), F32)] * 2,
        scratch_shapes=[pltpu.VMEM((2, RET_H, LANES, RET_DV), F32), pltpu.VMEM((2, RET_H, SCAN_CHUNK, SCAN_CHUNK), F32),
                        pltpu.VMEM((2, RET_H, SCAN_CHUNK, LANES), BF16), pltpu.VMEM((2, RET_H, SCAN_CHUNK, RET_DV), BF16)],
        compiler_params=_cparams(("parallel", "arbitrary")), name="ret_scan",
    )(rq, rk, rv, rq, rk, rv, logit)


def _post_ssm_kernel(x_ref, yf_ref, yb_ref, xs_ref, z_ref, rf_ref, rb_ref, rg_ref, mod_ref, dsk_ref, sn_ref, rn_ref,
                     wo_ref, n2_ref, w13_ref, w2_ref, o_ref):
    z = z_ref[0]
    y = (yf_ref[0] + yb_ref[0] + dsk_ref[...] * xs_ref[0]) * (z * jax.nn.sigmoid(z))
    gw = SSD_DIN // SSD_NG
    o = None
    for g in range(SSD_NG):
        yg = (_rms(y[:, gw * g:gw * (g + 1)]) * sn_ref[:, gw * g:gw * (g + 1)]).astype(BF16)
        part = jnp.dot(yg, wo_ref[gw * g:gw * (g + 1), :], preferred_element_type=F32)
        o = part if o is None else o + part
    rg = rg_ref[0]
    gate = rg * jax.nn.sigmoid(rg)
    for hh in range(RET_H):
        sl = slice(RET_DV * hh, RET_DV * (hh + 1))
        yr = rf_ref[0, :, sl] + rb_ref[0, :, sl]
        mu = jnp.mean(yr, axis=-1, keepdims=True)
        dev = yr - mu
        var = jnp.mean(dev * dev, axis=-1, keepdims=True)
        yn = (dev * lax.rsqrt(var + EPS) * rn_ref[:, sl] * gate[:, sl]).astype(BF16)
        o = o + jnp.dot(yn, wo_ref[SSD_DIN + RET_DV * hh:SSD_DIN + RET_DV * (hh + 1), :], preferred_element_type=F32)
    x1 = x_ref[0] + mod_ref[0, 0, 2:3, :] * o
    o_ref[0] = _swiglu_tail(x1, mod_ref, n2_ref, w13_ref, w2_ref)


def _post_ssm(xj, yf, yb, xs, z, rf, rb, rg, mods, dsk, sn, rn, wo, n2, w13, w2, ctx_len):
    b, n, d = xj.shape
    off = ctx_len // ROW_TILE
    nt = n // ROW_TILE - off
    lat = lambda w: pl.BlockSpec((1, ROW_TILE, w), lambda i, t: (i, t + off, 0))
    return pl.pallas_call(
        _post_ssm_kernel,
        grid=(b, nt),
        in_specs=[lat(d)] * 8 + [pl.BlockSpec((1, 1, 6, d), lambda i, t: (i, 1, 0, 0)),
                                 _const_spec(dsk.shape), _const_spec(sn.shape), _const_spec(rn.shape),
                                 _const_spec(wo.shape), _const_spec(n2.shape), _const_spec(w13.shape),
                                 _const_spec(w2.shape)],
        out_specs=pl.BlockSpec((1, ROW_TILE, d), lambda i, t: (i, t, 0)),
        out_shape=jax.ShapeDtypeStruct((b, nt * ROW_TILE, d), F32),
        compiler_params=_cparams(("parallel", "arbitrary")), name="post_ssm",
    )(xj, yf, yb, xs, z, rf, rb, rg, mods, dsk, sn, rn, wo, n2, w13, w2)


def _ssm_layer_params(w_in, conv_w, conv_b, dt_bias, a_log, d_skip, ssd_norm, ret_logit, ret_norm, w_out):
    z, xbc, dt, rq, rk, rv, rg = jnp.split(
        w_in, np.cumsum([SSD_DIN, XBC, 2 * SSD_NH, RET_H * RET_DK, RET_H * RET_DK, RET_H * RET_DV]).tolist(), axis=1)
    dt = jnp.pad(dt, ((0, 0), (0, LANES - 2 * SSD_NH)))
    win = jnp.concatenate([z, xbc, rq, rk, rv, rg, dt], axis=1).astype(BF16)
    pad_row = lambda a: jnp.pad(a.reshape(-1), (0, LANES - a.size)).reshape(1, LANES)
    return dict(win=win, cw=conv_w, cb=conv_b.reshape(1, -1), dtb=pad_row(dt_bias), alog=pad_row(a_log),
                dsk=jnp.repeat(d_skip, SSD_P).reshape(1, -1), sn=ssd_norm.reshape(1, -1), logit=pad_row(ret_logit),
                rn=ret_norm.reshape(1, -1), wo=w_out.astype(BF16))


def _ssm_layer(xj, mods, prm, tabs, n1, n2, w13, w2, ctx_len):
    z, xs, bc, dt, rq, rk, rv, rg = _pre_ssm(xj, mods, n1, prm['win'], prm['cw'], prm['cb'], prm['dtb'],
                                             tabs[0], tabs[1], ctx_len)
    yf, yb = _ssd_scan(xs, bc, dt, prm['alog'], ctx_len)
    rf, rb = _ret_scan(rq, rk, rv, prm['logit'], ctx_len)
    return _post_ssm(xj, yf, yb, xs, z, rf, rb, rg, mods, prm['dsk'], prm['sn'], prm['rn'], prm['wo'], n2, w13, w2,
                     ctx_len)


def _rope_angles(seq_len, ctx_len, dim):
    rows = seq_len // GRID_W
    rr, cc = jnp.meshgrid(jnp.arange(rows, dtype=F32), jnp.arange(GRID_W, dtype=F32), indexing='ij')
    quarter = dim // 4
    inv = ROPE_THETA ** (-jnp.arange(quarter, dtype=F32) / quarter)
    ang = jnp.concatenate([rr.reshape(-1)[:, None] * inv, cc.reshape(-1)[:, None] * inv], axis=-1)
    cos = jnp.concatenate([jnp.ones((ctx_len, dim // 2), F32), jnp.cos(ang)], axis=0)
    sin = jnp.concatenate([jnp.zeros((ctx_len, dim // 2), F32), jnp.sin(ang)], axis=0)
    return cos, sin


def _interleaved_tables(cos, sin):
    c = jnp.repeat(cos, 2, axis=-1)
    s = jnp.stack([-sin, sin], axis=-1).reshape(sin.shape[0], -1)
    return c, s


def _attn_layer_params(w_in, gqa_qn, gqa_kn, mla_qa_norm, mla_wq_b, mla_kva_norm, mla_wkv_b, mla_qn, mla_kn, w_out):
    d = w_in.shape[0]
    qa, ka, va, ql, kvl, kr = jnp.split(w_in, np.cumsum([512, 128, 128, MLA_Q_RANK, MLA_KV_RANK]).tolist(), axis=1)
    qa = qa.reshape(d, GQA_HQ, HEAD_DIM)
    qa = jnp.concatenate([jnp.concatenate([qa[:, j], qa[:, j + 4]], axis=1) for j in range(4)], axis=1)
    kr = jnp.pad(kr, ((0, 0), (MLA_DNOPE, LANES - MLA_DQK)))
    win = jnp.concatenate([qa, ka, va, ql, kvl, kr], axis=1).astype(BF16)
    wqb = jnp.pad(mla_wq_b.reshape(MLA_Q_RANK, MLA_H, MLA_DQK), ((0, 0), (0, 0), (0, LANES - MLA_DQK)))
    wqb = wqb.reshape(MLA_Q_RANK, MLA_H * LANES).astype(BF16)
    wkv = mla_wkv_b.reshape(MLA_KV_RANK, MLA_H, MLA_DNOPE + HEAD_DIM)
    wk = jnp.pad(wkv[:, :, :MLA_DNOPE], ((0, 0), (0, 0), (0, LANES - MLA_DNOPE))).reshape(MLA_KV_RANK, MLA_H * LANES)
    wv = wkv[:, :, MLA_DNOPE:].reshape(MLA_KV_RANK, MLA_H * HEAD_DIM)
    pad_gain = lambda g: jnp.pad(g, (0, LANES - MLA_DQK)).reshape(1, LANES)
    woa = w_out[:512].reshape(GQA_HQ, HEAD_DIM, d)
    woa = jnp.concatenate([jnp.concatenate([woa[j], woa[j + 4]], axis=0) for j in range(4)], axis=0)
    wo = jnp.concatenate([woa, w_out[512:]], axis=0).astype(BF16)
    return dict(win=win, gq=jnp.tile(gqa_qn, 2).reshape(1, LANES), gk=jnp.tile(gqa_kn, 2).reshape(1, LANES),
                gqa=mla_qa_norm.reshape(1, -1), gkva=mla_kva_norm.reshape(1, -1), wqb=wqb, wk=wk.astype(BF16),
                wv=wv.astype(BF16), gmq=pad_gain(mla_qn), gmk=pad_gain(mla_kn), wo=wo)


def _attn_tables(seq_len, ctx_len):
    cg, sg = _interleaved_tables(*_rope_angles(seq_len, ctx_len, HEAD_DIM))
    cg, sg = jnp.tile(cg, (1, 2)), jnp.tile(sg, (1, 2))
    cm, sm = _interleaved_tables(*_rope_angles(seq_len, ctx_len, MLA_DROPE))
    n = cm.shape[0]
    cm = jnp.concatenate([jnp.ones((n, MLA_DNOPE), F32), cm, jnp.ones((n, LANES - MLA_DQK), F32)], axis=1)
    sm = jnp.pad(sm, ((0, 0), (MLA_DNOPE, LANES - MLA_DQK)))
    return cg, sg, cm, sm


def _layer_mods(mods_i, batch):
    d = mods_i.shape[1] // 6
    lat = mods_i[:batch].reshape(batch, 1, 6, d)
    ctx = jnp.broadcast_to(mods_i[batch].reshape(1, 1, 6, d), (batch, 1, 6, d))
    return jnp.concatenate([ctx, lat], axis=1)


def _attention_layer(ctx, x, mods, prm, tabs, n1, n2, w13, w2):
    ctx_len = ctx.shape[1]
    qg, kg, vglo, vghi, qm, km, vmlo, vmhi = _pre_attn(
        ctx, x, mods, n1, prm['win'], prm['gq'], prm['gk'], prm['gqa'], prm['gkva'], prm['wqb'], prm['wk'], prm['wv'],
        prm['gmq'], prm['gmk'], *tabs)
    oa = _attention(qg, kg, vglo, vghi, shared_k=True, ctx_len=ctx_len)
    ob = _attention(qm, km, vmlo, vmhi, shared_k=False, ctx_len=ctx_len)
    return _post_attn(ctx, x, oa, ob, mods, prm['wo'], n2, w13, w2)


def _all_mods(c, c_ctx, mod_w, mod_b):
    batch = c.shape[0]
    rows = -(-(batch + 1) // 8) * 8
    cvec = jnp.concatenate([c, c_ctx[None, :], jnp.zeros((rows - batch - 1, c.shape[1]), F32)], axis=0)
    mods = _modulation(cvec, mod_w, mod_b)
    return [_layer_mods(mods[i], batch) for i in range(mod_w.shape[0])]


def kernel(x, c, ctx, c_ctx, mod_w, mod_b, norm1_w, norm2_w, ffn_w13, ffn_w2, attn_w_in, gqa_qn, gqa_kn, mla_qa_norm, mla_wq_b, mla_kva_norm, mla_wkv_b, mla_qn, mla_kn, attn_w_out, ssm_w_in, ssd_conv_w, ssd_conv_b, ssd_dt_bias, ssd_a_log, ssd_d, ssd_norm, ret_decay_logit, ret_norm, ssm_w_out):
    ctx_len, seq_len = ctx.shape[1], x.shape[1]
    assert ctx_len % ROW_TILE == 0 and seq_len % ROW_TILE == 0 and seq_len % GRID_W == 0
    assert mod_w.shape[0] == 2, "one attention layer followed by one SSM layer"
    mods = _all_mods(c, c_ctx, mod_w, mod_b)
    tabs = _attn_tables(seq_len, ctx_len)
    prm0 = _attn_layer_params(attn_w_in[0], gqa_qn[0], gqa_kn[0], mla_qa_norm[0], mla_wq_b[0], mla_kva_norm[0],
                              mla_wkv_b[0], mla_qn[0], mla_kn[0], attn_w_out[0])
    xj = _attention_layer(ctx, x, mods[0], prm0, tabs, norm1_w[0].reshape(1, -1), norm2_w[0].reshape(1, -1),
                          ffn_w13[0].astype(BF16), ffn_w2[0].astype(BF16))
    prm1 = _ssm_layer_params(ssm_w_in[0], ssd_conv_w[0], ssd_conv_b[0], ssd_dt_bias[0], ssd_a_log[0], ssd_d[0],
                             ssd_norm[0], ret_decay_logit[0], ret_norm[0], ssm_w_out[0])
    return _ssm_layer(xj, mods[1], prm1, tabs, norm1_w[1].reshape(1, -1), norm2_w[1].reshape(1, -1),
                      ffn_w13[1].astype(BF16), ffn_w2[1].astype(BF16), ctx_len)


def _debug_layer1(x, c, ctx, c_ctx, mod_w, mod_b, norm1_w, norm2_w, ffn_w13, ffn_w2, ssm_w_in, ssd_conv_w, ssd_conv_b,
                  ssd_dt_bias, ssd_a_log, ssd_d, ssd_norm, ret_decay_logit, ret_norm, ssm_w_out, **_):
    ctx_len, seq_len = ctx.shape[1], x.shape[1]
    mods = _all_mods(c, c_ctx, mod_w, mod_b)
    tabs = _attn_tables(seq_len, ctx_len)
    xj = jnp.concatenate([ctx, x], axis=1)
    prm1 = _ssm_layer_params(ssm_w_in[0], ssd_conv_w[0], ssd_conv_b[0], ssd_dt_bias[0], ssd_a_log[0], ssd_d[0],
                             ssd_norm[0], ret_decay_logit[0], ret_norm[0], ssm_w_out[0])
    return _ssm_layer(xj, mods[1], prm1, tabs, norm1_w[1].reshape(1, -1), norm2_w[1].reshape(1, -1),
                      ffn_w13[1].astype(BF16), ffn_w2[1].astype(BF16), ctx_len)


def _debug_layer0(x, c, ctx, c_ctx, mod_w, mod_b, norm1_w, norm2_w, ffn_w13, ffn_w2, attn_w_in, gqa_qn, gqa_kn,
                  mla_qa_norm, mla_wq_b, mla_kva_norm, mla_wkv_b, mla_qn, mla_kn, attn_w_out, **_):
    ctx_len, seq_len = ctx.shape[1], x.shape[1]
    mods = _all_mods(c, c_ctx, mod_w, mod_b)
    prm = _attn_layer_params(attn_w_in[0], gqa_qn[0], gqa_kn[0], mla_qa_norm[0], mla_wq_b[0], mla_kva_norm[0],
                             mla_wkv_b[0], mla_qn[0], mla_kn[0], attn_w_out[0])
    return _attention_layer(ctx, x, mods[0], prm, _attn_tables(seq_len, ctx_len), norm1_w[0].reshape(1, -1),
                            norm2_w[0].reshape(1, -1), ffn_w13[0].astype(BF16), ffn_w2[0].astype(BF16))
```

```python
import functools
import math

import jax
import jax.numpy as jnp
import numpy as np
from jax import lax
from jax.experimental import pallas as pl
from jax.experimental.pallas import tpu as pltpu

F32 = jnp.float32
BF16 = jnp.bfloat16

EPS = 1e-6
ROPE_THETA = 10000.0
GRID_W = 64
HEAD_DIM = 64
LANES = 128
ROW_TILE = 256
SCAN_CHUNK = 128
Q_TILE = 1024
VMEM_LIMIT = 56 * 1024 * 1024

GQA_HQ = 8
GQA_HKV = 2
MLA_H = 8
MLA_DNOPE = 64
MLA_DROPE = 32
MLA_DQK = MLA_DNOPE + MLA_DROPE
MLA_Q_RANK = 384
MLA_KV_RANK = 256
LOG2E = 1.4426950408889634


def _cparams(sem):
    return pltpu.CompilerParams(dimension_semantics=sem, vmem_limit_bytes=VMEM_LIMIT)


def _const_spec(shape):
    nd = len(shape)
    return pl.BlockSpec(shape, lambda *_: (0,) * nd, pipeline_mode=pl.Buffered(1))


def _rms(x):
    return x * lax.rsqrt(jnp.mean(x * x, axis=-1, keepdims=True) + EPS)


def _swap_pairs(y):
    lane = lax.broadcasted_iota(jnp.int32, y.shape, y.ndim - 1)
    nxt = pltpu.roll(y, LANES - 1, y.ndim - 1)
    prv = pltpu.roll(y, 1, y.ndim - 1)
    return jnp.where(lane % 2 == 0, nxt, prv)


def _rope(y, cos, sin_signed):
    return y * cos + _swap_pairs(y) * sin_signed


def _mod_kernel(c_ref, w_ref, b_ref, o_ref):
    c = c_ref[...]
    a = (c * jax.nn.sigmoid(c)).astype(BF16)
    o_ref[0] = jnp.dot(a, w_ref[0].astype(BF16), preferred_element_type=F32) + b_ref[0]


def _modulation(cvec, mod_w, mod_b):
    depth, d, n = mod_w.shape
    rows = cvec.shape[0]
    tn = 512
    return pl.pallas_call(
        _mod_kernel,
        grid=(depth, n // tn),
        in_specs=[pl.BlockSpec((rows, d), lambda i, j: (0, 0)),
                  pl.BlockSpec((1, d, tn), lambda i, j: (i, 0, j)),
                  pl.BlockSpec((1, 1, tn), lambda i, j: (i, 0, j))],
        out_specs=pl.BlockSpec((1, rows, tn), lambda i, j: (i, 0, j)),
        out_shape=jax.ShapeDtypeStruct((depth, rows, n), F32),
        compiler_params=_cparams(("arbitrary", "arbitrary")), name="modulation",
    )(cvec, mod_w, mod_b.reshape(depth, 1, n))


def _pre_attn_kernel(ctx_ref, x_ref, mod_ref, n1_ref, win_ref, gq_ref, gk_ref, gqa_ref, gkva_ref, wqb_ref, wk_ref,
                     wv_ref, gmq_ref, gmk_ref, cg_ref, sg_ref, cm_ref, sm_ref,
                     qg_ref, kg_ref, vglo_ref, vghi_ref, qm_ref, km_ref, vmlo_ref, vmhi_ref, *, ctx_tiles):
    x = jnp.where(pl.program_id(1) < ctx_tiles, ctx_ref[0], x_ref[0])
    shift = mod_ref[0, 0, 0:1, :]
    scale = mod_ref[0, 0, 1:2, :]
    u = _rms(x) * n1_ref[...] * (1.0 + scale) + shift
    h = jnp.dot(u.astype(BF16), win_ref[...], preferred_element_type=F32)

    tm = x.shape[0]
    lane = lax.broadcasted_iota(jnp.int32, (tm, LANES), 1)
    lo = lane < HEAD_DIM
    cg, sg, cm, sm = cg_ref[...], sg_ref[...], cm_ref[...], sm_ref[...]

    def pair_norm_rope(blk, gain):
        sq = blk * blk
        ss_lo = jnp.sum(jnp.where(lo, sq, 0.0), axis=-1, keepdims=True)
        ss_hi = jnp.sum(jnp.where(lo, 0.0, sq), axis=-1, keepdims=True)
        r = lax.rsqrt(jnp.where(lo, ss_lo, ss_hi) * (1.0 / HEAD_DIM) + EPS)
        return _rope(blk * r * gain, cg, sg)

    def head_norm_rope(blk, gain):
        r = lax.rsqrt(jnp.sum(blk * blk, axis=-1, keepdims=True) * (1.0 / MLA_DQK) + EPS)
        return _rope(blk * r * gain, cm, sm)

    ql = (_rms(h[:, 768:1152]) * gqa_ref[...]).astype(BF16)
    qm = jnp.dot(ql, wqb_ref[...], preferred_element_type=F32)
    kvl = (_rms(h[:, 1152:1408]) * gkva_ref[...]).astype(BF16)
    kn = jnp.dot(kvl, wk_ref[...], preferred_element_type=F32)
    vm = jnp.dot(kvl, wv_ref[...], preferred_element_type=F32)

    qscale = (HEAD_DIM ** -0.5) * LOG2E
    for j in range(4):
        y = pair_norm_rope(h[:, LANES * j:LANES * (j + 1)], gq_ref[...]) * qscale
        qg_ref[0, :, LANES * (2 * j):LANES * (2 * j + 1)] = jnp.where(lo, y, 0.0).astype(BF16)
        qg_ref[0, :, LANES * (2 * j + 1):LANES * (2 * j + 2)] = jnp.where(lo, 0.0, y).astype(BF16)
    kg_ref[0] = pair_norm_rope(h[:, 512:640], gk_ref[...]).astype(BF16)

    def put_values(blk, lo_ref, hi_ref, sl):
        lo_ref[0, :, sl] = jnp.where(lo, blk, jnp.where(lane == HEAD_DIM, 1.0, 0.0)).astype(BF16)
        hi_ref[0, :, sl] = jnp.where(lo, jnp.where(lane == 0, 1.0, 0.0), blk).astype(BF16)

    put_values(h[:, 640:768], vglo_ref, vghi_ref, slice(0, LANES))

    for j in range(MLA_H // 2):
        put_values(vm[:, LANES * j:LANES * (j + 1)], vmlo_ref, vmhi_ref, slice(LANES * j, LANES * (j + 1)))
    kr = h[:, 1408:1536]
    mscale = (MLA_DQK ** -0.5) * LOG2E
    for hh in range(MLA_H):
        sl = slice(LANES * hh, LANES * (hh + 1))
        qm_ref[0, :, sl] = (head_norm_rope(qm[:, sl], gmq_ref[...]) * mscale).astype(BF16)
        km_ref[0, :, sl] = head_norm_rope(kn[:, sl] + kr, gmk_ref[...]).astype(BF16)


def _ctx_lat_specs(ct, d):
    return [pl.BlockSpec((1, ROW_TILE, d), lambda i, t: (i, jnp.minimum(t, ct - 1), 0)),
            pl.BlockSpec((1, ROW_TILE, d), lambda i, t: (i, jnp.maximum(t - ct, 0), 0))]


def _pre_attn(ctx, x, mods, n1, win, gq, gk, gqa, gkva, wqb, wk, wv, gmq, gmk, cg, sg, cm, sm):
    b, _, d = x.shape
    ct = ctx.shape[1] // ROW_TILE
    n = ctx.shape[1] + x.shape[1]
    nt = n // ROW_TILE
    row = lambda w: pl.BlockSpec((1, ROW_TILE, w), lambda i, t: (i, t, 0))
    qrow = lambda w: pl.BlockSpec((1, ROW_TILE, w), lambda i, t: (i, (t + nt - ct) % nt, 0))
    tab = pl.BlockSpec((ROW_TILE, LANES), lambda i, t: (t, 0))
    outs = [(8 * LANES, BF16), (LANES, BF16), (LANES, BF16), (LANES, BF16),
            (8 * LANES, BF16), (8 * LANES, BF16), (4 * LANES, BF16), (4 * LANES, BF16)]
    return pl.pallas_call(
        functools.partial(_pre_attn_kernel, ctx_tiles=ct),
        grid=(b, nt),
        in_specs=_ctx_lat_specs(ct, d) + [
                  pl.BlockSpec((1, 1, 6, d), lambda i, t: (i, jnp.minimum(t, 1), 0, 0)),
                  _const_spec(n1.shape), _const_spec(win.shape), _const_spec(gq.shape), _const_spec(gk.shape),
                  _const_spec(gqa.shape), _const_spec(gkva.shape), _const_spec(wqb.shape), _const_spec(wk.shape),
                  _const_spec(wv.shape), _const_spec(gmq.shape), _const_spec(gmk.shape), tab, tab, tab, tab],
        out_specs=[(qrow if i in (0, 4) else row)(w) for i, (w, _) in enumerate(outs)],
        out_shape=[jax.ShapeDtypeStruct((b, n, w), dt) for w, dt in outs],
        compiler_params=_cparams(("parallel", "arbitrary")), name="pre_attn",
    )(ctx, x, mods, n1, win, gq, gk, gqa, gkva, wqb, wk, wv, gmq, gmk, cg, sg, cm, sm)


def _attn_kernel(q_ref, k_ref, vlo_ref, vhi_ref, o_ref, *, shared_k, ctx_len):
    t = pl.program_id(2)
    last = pl.num_programs(2) - 1

    def run(nq, nk):
        s, acc = [], []
        for idx in range(2):
            q = q_ref[0, :nq, LANES * idx:LANES * (idx + 1)]
            koff = 0 if shared_k else LANES * idx
            s.append(lax.dot_general(q, k_ref[0, :nk, koff:koff + LANES], (((1,), (1,)), ((), ())),
                                     preferred_element_type=F32))
        for idx, v_ref in enumerate((vlo_ref, vhi_ref)):
            p = jnp.exp2(s[idx] - jnp.max(s[idx], axis=-1, keepdims=True)).astype(BF16)
            acc.append(jnp.dot(p, v_ref[0, :nk, :], preferred_element_type=F32))
        out0 = acc[0] / acc[0][:, HEAD_DIM:HEAD_DIM + 1]
        out1 = acc[1] / acc[1][:, 0:1]
        lane = lax.broadcasted_iota(jnp.int32, out0.shape, 1)
        o_ref[0, :nq, :] = jnp.where(lane < HEAD_DIM, out0, out1).astype(o_ref.dtype)

    @pl.when(t < last)
    def _():
        run(q_ref.shape[1], k_ref.shape[1])

    @pl.when(t == last)
    def _():
        run(ctx_len, ctx_len)
        if ctx_len < q_ref.shape[1]:
            o_ref[0, ctx_len:, :] = jnp.zeros((q_ref.shape[1] - ctx_len, o_ref.shape[2]), o_ref.dtype)


def _attention(q, k, vlo, vhi, *, shared_k, ctx_len):
    b, n, _ = q.shape
    assert (n - ctx_len) % Q_TILE == 0 and ctx_len <= Q_TILE
    nt = (n - ctx_len) // Q_TILE + 1
    pairs = q.shape[2] // (2 * LANES)
    kw = LANES if shared_k else 2 * LANES
    kmap = (lambda i, j, t: (i, 0, 0)) if shared_k else (lambda i, j, t: (i, 0, j))
    return pl.pallas_call(
        functools.partial(_attn_kernel, shared_k=shared_k, ctx_len=ctx_len),
        grid=(b, pairs, nt),
        in_specs=[pl.BlockSpec((1, Q_TILE, 2 * LANES), lambda i, j, t: (i, t, j)),
                  pl.BlockSpec((1, n, kw), kmap),
                  pl.BlockSpec((1, n, LANES), kmap),
                  pl.BlockSpec((1, n, LANES), kmap)],
        out_specs=pl.BlockSpec((1, Q_TILE, LANES), lambda i, j, t: (i, t, j)),
        out_shape=jax.ShapeDtypeStruct((b, n, pairs * LANES), BF16),
        compiler_params=_cparams(("parallel", "parallel", "arbitrary")),
        name="attn_gqa" if shared_k else "attn_mla",
    )(q, k, vlo, vhi)


def _swiglu_tail(x1, mod_ref, n2_ref, w13_ref, w2_ref):
    shift, scale, gate = mod_ref[0, 0, 3:4, :], mod_ref[0, 0, 4:5, :], mod_ref[0, 0, 5:6, :]
    f_in = _rms(x1) * n2_ref[...] * (1.0 + scale) + shift
    h = jnp.dot(f_in.astype(BF16), w13_ref[...], preferred_element_type=F32)
    hid = h.shape[1] // 2
    gte, up = h[:, :hid], h[:, hid:]
    a = (gte * jax.nn.sigmoid(gte) * up).astype(BF16)
    return x1 + gate * jnp.dot(a, w2_ref[...], preferred_element_type=F32)


def _post_attn_kernel(ctx_ref, x_ref, oa_ref, ob_ref, mod_ref, wo_ref, n2_ref, w13_ref, w2_ref, y_ref, *, ctx_tiles):
    half = oa_ref.shape[2]
    o = jnp.dot(oa_ref[0], wo_ref[:half, :], preferred_element_type=F32)
    o = o + jnp.dot(ob_ref[0], wo_ref[half:, :], preferred_element_type=F32)
    x = jnp.where(pl.program_id(1) < ctx_tiles, ctx_ref[0], x_ref[0])
    x1 = x + mod_ref[0, 0, 2:3, :] * o
    y_ref[0] = _swiglu_tail(x1, mod_ref, n2_ref, w13_ref, w2_ref)


def _post_attn(ctx, x, oa, ob, mods, wo, n2, w13, w2):
    b, _, d = x.shape
    ct = ctx.shape[1] // ROW_TILE
    n = ctx.shape[1] + x.shape[1]
    nt = n // ROW_TILE
    row = lambda w: pl.BlockSpec((1, ROW_TILE, w), lambda i, t: (i, t, 0))
    qrow = lambda w: pl.BlockSpec((1, ROW_TILE, w), lambda i, t: (i, (t + nt - ct) % nt, 0))
    return pl.pallas_call(
        functools.partial(_post_attn_kernel, ctx_tiles=ct),
        grid=(b, nt),
        in_specs=_ctx_lat_specs(ct, d) + [
                  qrow(oa.shape[2]), qrow(ob.shape[2]),
                  pl.BlockSpec((1, 1, 6, d), lambda i, t: (i, jnp.minimum(t, 1), 0, 0)),
                  _const_spec(wo.shape), _const_spec(n2.shape), _const_spec(w13.shape), _const_spec(w2.shape)],
        out_specs=row(d),
        out_shape=jax.ShapeDtypeStruct((b, n, d), F32),
        compiler_params=_cparams(("parallel", "arbitrary")), name="post_attn",
    )(ctx, x, oa, ob, mods, wo, n2, w13, w2)


HALO = 8
SSD_DIN = 1024
SSD_NH = 16
SSD_NG = 2
SSD_NS = 128
SSD_P = 64
SSD_CONV = 5
XBC = SSD_DIN + 2 * SSD_NG * SSD_NS
RET_H = 8
RET_DK = 64
RET_DV = 128
C_Z, C_XBC, C_RQ, C_RK, C_RV, C_RG, C_DT, C_END = 0, 1024, 2560, 3072, 3584, 4608, 5632, 5760


def _softplus(x):
    return jnp.maximum(x, 0.0) + jnp.log1p(jnp.exp(-jnp.abs(x)))


def _pre_ssm_kernel(xp_ref, x_ref, xn_ref, mod_ref, n1_ref, win_ref, cw_ref, cb_ref, dtb_ref, cg_ref, sg_ref,
                    z_ref, xs_ref, bc_ref, dt_ref, rq_ref, rk_ref, rv_ref, rg_ref, ext_ref, *, ctx_tiles):
    t = pl.program_id(1)
    nt = pl.num_programs(1)
    tm = x_ref.shape[1]
    x_ext = jnp.concatenate([xp_ref[0], x_ref[0], xn_ref[0]], axis=0)
    u = _rms(x_ext) * n1_ref[...] * (1.0 + mod_ref[0, 0, 1:2, :]) + mod_ref[0, 0, 0:1, :]
    h = jnp.dot(u.astype(BF16), win_ref[...], preferred_element_type=F32)
    hm = h[HALO:HALO + tm]
    z_ref[0] = hm[:, C_Z:C_XBC]
    rg_ref[0] = hm[:, C_RG:C_DT]
    rv_ref[0] = hm[:, C_RV:C_RG].astype(BF16)
    dt_ref[0] = _softplus(hm[:, C_DT:C_END] + dtb_ref[...])

    rows = lax.broadcasted_iota(jnp.int32, (tm + 2 * HALO, 1), 0)
    has_prev = jnp.logical_and(t != 0, t != ctx_tiles)
    has_next = jnp.logical_and(t != ctx_tiles - 1, t != nt - 1)
    keep = jnp.logical_and(jnp.logical_or(rows >= HALO, has_prev), jnp.logical_or(rows < HALO + tm, has_next))
    ext_ref[...] = jnp.where(keep, h[:, C_XBC:C_RQ], 0.0)
    acc = cb_ref[...] + cw_ref[0:1, :] * ext_ref[pl.ds(HALO - 2, tm), :]
    for k in range(1, SSD_CONV):
        acc = acc + cw_ref[k:k + 1, :] * ext_ref[pl.ds(HALO - 2 + k, tm), :]
    act = acc * jax.nn.sigmoid(acc)
    xs_ref[0] = act[:, :SSD_DIN]
    bc_ref[0] = act[:, SSD_DIN:].astype(BF16)

    cg, sg = cg_ref[...], sg_ref[...]
    for j in range(RET_H * RET_DK // LANES):
        sl = slice(LANES * j, LANES * (j + 1))
        rq_ref[0, :, sl] = _rope(hm[:, C_RQ + LANES * j:C_RQ + LANES * (j + 1)], cg, sg).astype(BF16)
        rk = _rope(hm[:, C_RK + LANES * j:C_RK + LANES * (j + 1)], cg, sg) * (RET_DK ** -0.5)
        rk_ref[0, :, sl] = rk.astype(BF16)


def _pre_ssm(xj, mods, n1, win, cw, cb, dtb, cg, sg, ctx_len):
    b, n, d = xj.shape
    nt = n // ROW_TILE
    per = ROW_TILE // HALO
    row = lambda w: pl.BlockSpec((1, ROW_TILE, w), lambda i, t: (i, t, 0))
    tab = pl.BlockSpec((ROW_TILE, LANES), lambda i, t: (t, 0))
    outs = [(SSD_DIN, F32), (SSD_DIN, F32), (2 * SSD_NG * SSD_NS, BF16), (LANES, F32), (RET_H * RET_DK, BF16),
            (RET_H * RET_DK, BF16), (RET_H * RET_DV, BF16), (RET_H * RET_DV, F32)]
    return pl.pallas_call(
        functools.partial(_pre_ssm_kernel, ctx_tiles=ctx_len // ROW_TILE),
        grid=(b, nt),
        in_specs=[pl.BlockSpec((1, HALO, d), lambda i, t: (i, jnp.maximum(t * per - 1, 0), 0)),
                  row(d),
                  pl.BlockSpec((1, HALO, d), lambda i, t: (i, jnp.minimum((t + 1) * per, n // HALO - 1), 0)),
                  pl.BlockSpec((1, 1, 6, d), lambda i, t: (i, jnp.minimum(t, 1), 0, 0)),
                  _const_spec(n1.shape), _const_spec(win.shape), _const_spec(cw.shape), _const_spec(cb.shape),
                  _const_spec(dtb.shape), tab, tab],
        out_specs=[row(w) for w, _ in outs],
        out_shape=[jax.ShapeDtypeStruct((b, n, w), dt) for w, dt in outs],
        scratch_shapes=[pltpu.VMEM((ROW_TILE + 2 * HALO, XBC), F32)],
        compiler_params=_cparams(("parallel", "arbitrary")), name="pre_ssm",
    )(xj, xj, xj, mods, n1, win, cw, cb, dtb, cg, sg)


def _split_dot_rhs(m, x, parts):
    acc = None
    for _ in range(parts):
        p = x.astype(BF16)
        d = jnp.dot(m, p, preferred_element_type=F32)
        acc = d if acc is None else acc + d
        x = x - p.astype(F32)
    return acc


def _split_dot_lhs(x, m, parts):
    acc = None
    for _ in range(parts):
        p = x.astype(BF16)
        d = jnp.dot(p, m, preferred_element_type=F32)
        acc = d if acc is None else acc + d
        x = x - p.astype(F32)
    return acc


def _running_sum_rows(x, reverse):
    n = x.shape[0]
    row = lax.broadcasted_iota(jnp.int32, x.shape, 0)
    k = 1
    while k < n:
        if reverse:
            x = x + jnp.where(row < n - k, pltpu.roll(x, n - k, 0), 0.0)
        else:
            x = x + jnp.where(row >= k, pltpu.roll(x, k, 0), 0.0)
        k *= 2
    return x


def _bwd_chunk(s, nchunks, ctx_chunks):
    return jnp.where(s < ctx_chunks, ctx_chunks - 1 - s, nchunks - 1 - (s - ctx_chunks))


def _scan_masks():
    i = np.arange(SCAN_CHUNK)
    tri = np.stack([i[None, :] <= i[:, None], i[None, :] >= i[:, None]])
    lane, col = np.arange(LANES)[:, None], np.arange(SSD_DIN)[None, :] // SSD_P
    expand = np.stack([col + SSD_NH * d == lane for d in range(2)])
    return jnp.asarray(tri, BF16), jnp.asarray(expand, BF16)


def _ssd_kernel(xsf_ref, bcf_ref, dtf_ref, xsb_ref, bcb_ref, dtb_ref, alog_ref, alogc_ref, tri_ref, exp_ref,
                yf_ref, yb_ref, st_ref):
    @pl.when(pl.program_id(1) == 0)
    def _():
        st_ref[...] = jnp.zeros_like(st_ref)

    q = SCAN_CHUNK
    rows = lax.broadcasted_iota(jnp.int32, (q, q), 0)
    cols = lax.broadcasted_iota(jnp.int32, (q, q), 1)
    lo = cols < SSD_P
    a_row = -jnp.exp(alog_ref[...])
    a_col = -jnp.exp(alogc_ref[...])
    gw = SSD_DIN // SSD_NG

    dirs = ((xsf_ref, bcf_ref, dtf_ref, yf_ref), (xsb_ref, bcb_ref, dtb_ref, yb_ref))
    tris = (cols <= rows, cols >= rows)
    cs, c_ts, vdes, xsbs, cdxs = [], [], [], [], []
    for d, (xs_ref, _, dt_ref, _) in enumerate(dirs):
        xs = xs_ref[0]
        dt = dt_ref[0]
        la = dt * a_row
        c = _running_sum_rows(la, reverse=(d == 1))
        ctot = jnp.sum(la, axis=0, keepdims=True)
        expand = exp_ref[d]
        stacked = jnp.concatenate([dt * jnp.exp(ctot - c), jnp.broadcast_to(jnp.exp(ctot), (8, LANES))], axis=0)
        ex = _split_dot_lhs(stacked, expand, 2)
        vdes.append((xs * ex[0:q]).astype(BF16))
        cdxs.append(ex[q:q + 1])
        xsbs.append(xs.astype(BF16))
        c2 = c * LOG2E
        cs.append(c2)
        c_ts.append((c2 - jnp.log2(dt)).T)

    scs, cgs, rhss = {}, {}, {}
    for g in range(SSD_NG):
        for d, (_, bc_ref, _, _) in enumerate(dirs):
            bg = bc_ref[0, :, SSD_NS * g:SSD_NS * (g + 1)]
            cgp = bc_ref[0, :, SSD_NG * SSD_NS + SSD_NS * g:SSD_NG * SSD_NS + SSD_NS * (g + 1)]
            scs[d, g] = lax.dot_general(cgp, bg, (((1,), (1,)), ((), ())), preferred_element_type=F32)
            cgs[d, g] = cgp
            st_prev = st_ref[d, g]
            st_ref[d, g] = st_prev * cdxs[d][:, gw * g:gw * (g + 1)] + jnp.dot(
                bg.astype(F32).T.astype(BF16), vdes[d][:, gw * g:gw * (g + 1)], preferred_element_type=F32)
            st_b = st_prev.astype(BF16)
            for pr in range(gw // LANES):
                rhss[d, g, pr] = jnp.concatenate([xsbs[d][:, gw * g + LANES * pr:gw * g + LANES * (pr + 1)],
                                                  st_b[:, LANES * pr:LANES * (pr + 1)]], axis=0)

    for g in range(SSD_NG):
        for pr in range(gw // LANES):
            for d, (_, _, _, y_ref) in enumerate(dirs):
                outs = []
                for e in range(2):
                    ln = SSD_NH * d + (SSD_NH // SSD_NG) * g + 2 * pr + e
                    c_i = jnp.broadcast_to(cs[d][:, ln:ln + 1], (q, q))
                    m = scs[d, g] * jnp.exp2(jnp.where(tris[d], c_i - c_ts[d][ln:ln + 1, :], -jnp.inf))
                    lhs = jnp.concatenate([m.astype(BF16), cgs[d, g] * jnp.exp2(c_i).astype(BF16)], axis=1)
                    outs.append(jnp.dot(lhs, rhss[d, g, pr], preferred_element_type=F32))
                sl = slice(gw * g + LANES * pr, gw * g + LANES * (pr + 1))
                y_ref[0, :, sl] = jnp.where(lo, outs[0], outs[1])


def _ssd_scan(xs, bc, dt, alog, ctx_len):
    b, n, _ = xs.shape
    nc = n // SCAN_CHUNK
    cc = ctx_len // SCAN_CHUNK
    fwd = lambda w: pl.BlockSpec((1, SCAN_CHUNK, w), lambda i, s: (i, s, 0))
    bwd = lambda w: pl.BlockSpec((1, SCAN_CHUNK, w), lambda i, s: (i, _bwd_chunk(s, nc, cc), 0))
    tri, expand = _scan_masks()
    return pl.pallas_call(
        _ssd_kernel,
        grid=(b, nc),
        in_specs=[fwd(xs.shape[2]), fwd(bc.shape[2]), fwd(dt.shape[2]),
                  bwd(xs.shape[2]), bwd(bc.shape[2]), bwd(dt.shape[2]), _const_spec(alog.shape),
                  _const_spec(alog.T.shape),
                  _const_spec(tri.shape), _const_spec(expand.shape)],
        out_specs=[fwd(SSD_DIN), bwd(SSD_DIN)],
        out_shape=[jax.ShapeDtypeStruct((b, n, SSD_DIN), F32)] * 2,
        scratch_shapes=[pltpu.VMEM((2, SSD_NG, SSD_NS, SSD_DIN // SSD_NG), F32)],
        compiler_params=_cparams(("parallel", "arbitrary")), name="ssd_scan",
    )(xs, bc, dt, xs, bc, dt, alog, alog.T, tri, expand)


def _ret_kernel(qf_ref, kf_ref, vf_ref, qb_ref, kb_ref, vb_ref, logit_ref, yf_ref, yb_ref,
                st_ref, dec_ref, qin_ref, eout_ref):
    q = SCAN_CHUNK
    rows = lax.broadcasted_iota(jnp.int32, (q, q), 0)
    cols = lax.broadcasted_iota(jnp.int32, (q, q), 1)
    lo = cols < RET_DK
    lo_rows = rows < RET_DK
    lg_row = -_softplus(-logit_ref[...])
    npairs = RET_H // 2

    @pl.when(pl.program_id(1) == 0)
    def _():
        st_ref[...] = jnp.zeros_like(st_ref)
        dist = jnp.abs(rows - cols).astype(F32)
        for d in range(2):
            tri = (cols <= rows) if d == 0 else (cols >= rows)
            steps_in = (rows + 1 if d == 0 else q - rows).astype(F32)
            steps_out = (q - 1 - rows if d == 0 else rows).astype(F32)
            for hh in range(RET_H):
                lg = lg_row[:, RET_H * d + hh:RET_H * d + hh + 1]
                dec_ref[d, hh] = jnp.exp(jnp.where(tri, dist * lg, -jnp.inf))
                own = lo if hh % 2 == 0 else jnp.logical_not(lo)
                qin_ref[d, hh] = jnp.where(own, jnp.exp(steps_in * lg), 0.0).astype(BF16)
                eout_ref[d, hh] = jnp.exp(steps_out * lg).astype(BF16)

    for d, (q_ref, k_ref, v_ref, y_ref) in enumerate(((qf_ref, kf_ref, vf_ref, yf_ref),
                                                       (qb_ref, kb_ref, vb_ref, yb_ref))):
        qps = [q_ref[0, :, LANES * pr:LANES * (pr + 1)] for pr in range(npairs)]
        kts = [k_ref[0, :, LANES * pr:LANES * (pr + 1)].astype(F32).T for pr in range(npairs)]
        scs = [jnp.dot(qps[pr], jnp.concatenate([jnp.where(lo_rows, kts[pr], 0.0), jnp.where(lo_rows, 0.0, kts[pr])],
                                                axis=1).astype(BF16), preferred_element_type=F32)
               for pr in range(npairs)]
        news = []
        for pr in range(npairs):
            vdes = [v_ref[0, :, RET_DV * (2 * pr + e):RET_DV * (2 * pr + e + 1)] * eout_ref[d, 2 * pr + e]
                    for e in range(2)]
            news.append(jnp.dot(kts[pr].astype(BF16), jnp.concatenate(vdes, axis=1), preferred_element_type=F32))
        for pr in range(npairs):
            for e in range(2):
                hh = 2 * pr + e
                m = scs[pr][:, q * e:q * (e + 1)] * dec_ref[d, hh]
                lhs = jnp.concatenate([m.astype(BF16), qps[pr] * qin_ref[d, hh]], axis=1)
                st_prev = st_ref[d, hh]
                rhs = jnp.concatenate([v_ref[0, :, RET_DV * hh:RET_DV * (hh + 1)], st_prev.astype(BF16)], axis=0)
                y_ref[0, :, RET_DV * hh:RET_DV * (hh + 1)] = jnp.dot(lhs, rhs, preferred_element_type=F32)
                gam = jnp.exp(q * lg_row[:, RET_H * d + hh:RET_H * d + hh + 1])
                st_ref[d, hh] = st_prev * gam + news[pr][:, RET_DV * e:RET_DV * (e + 1)]


def _ret_scan(rq, rk, rv, logit, ctx_len):
    b, n, _ = rq.shape
    nc = n // SCAN_CHUNK
    cc = ctx_len // SCAN_CHUNK
    fwd = lambda w: pl.BlockSpec((1, SCAN_CHUNK, w), lambda i, s: (i, s, 0))
    bwd = lambda w: pl.BlockSpec((1, SCAN_CHUNK, w), lambda i, s: (i, _bwd_chunk(s, nc, cc), 0))
    return pl.pallas_call(
        _ret_kernel,
        grid=(b, nc),
        in_specs=[fwd(rq.shape[2]), fwd(rk.shape[2]), fwd(rv.shape[2]),
                  bwd(rq.shape[2]), bwd(rk.shape[2]), bwd(rv.shape[2]), _const_spec(logit.shape)],
        out_specs=[fwd(RET_H * RET_DV), bwd(RET_H * RET_DV)],
        out_shape=[jax.ShapeDtypeStruct((b, n, RET_H * RET_DV), F32)] * 2,
        scratch_shapes=[pltpu.VMEM((2, RET_H, LANES, RET_DV), F32), pltpu.VMEM((2, RET_H, SCAN_CHUNK, SCAN_CHUNK), F32),
                        pltpu.VMEM((2, RET_H, SCAN_CHUNK, LANES), BF16), pltpu.VMEM((2, RET_H, SCAN_CHUNK, RET_DV), BF16)],
        compiler_params=_cparams(("parallel", "arbitrary")), name="ret_scan",
    )(rq, rk, rv, rq, rk, rv, logit)


def _post_ssm_kernel(x_ref, yf_ref, yb_ref, xs_ref, z_ref, rf_ref, rb_ref, rg_ref, mod_ref, dsk_ref, sn_ref, rn_ref,
                     wo_ref, n2_ref, w13_ref, w2_ref, o_ref):
    z = z_ref[0]
    y = (yf_ref[0] + yb_ref[0] + dsk_ref[...] * xs_ref[0]) * (z * jax.nn.sigmoid(z))
    gw = SSD_DIN // SSD_NG
    o = None
    for g in range(SSD_NG):
        yg = (_rms(y[:, gw * g:gw * (g + 1)]) * sn_ref[:, gw * g:gw * (g + 1)]).astype(BF16)
        part = jnp.dot(yg, wo_ref[gw * g:gw * (g + 1), :], preferred_element_type=F32)
        o = part if o is None else o + part
    rg = rg_ref[0]
    gate = rg * jax.nn.sigmoid(rg)
    for hh in range(RET_H):
        sl = slice(RET_DV * hh, RET_DV * (hh + 1))
        yr = rf_ref[0, :, sl] + rb_ref[0, :, sl]
        mu = jnp.mean(yr, axis=-1, keepdims=True)
        dev = yr - mu
        var = jnp.mean(dev * dev, axis=-1, keepdims=True)
        yn = (dev * lax.rsqrt(var + EPS) * rn_ref[:, sl] * gate[:, sl]).astype(BF16)
        o = o + jnp.dot(yn, wo_ref[SSD_DIN + RET_DV * hh:SSD_DIN + RET_DV * (hh + 1), :], preferred_element_type=F32)
    x1 = x_ref[0] + mod_ref[0, 0, 2:3, :] * o
    o_ref[0] = _swiglu_tail(x1, mod_ref, n2_ref, w13_ref, w2_ref)


def _post_ssm(xj, yf, yb, xs, z, rf, rb, rg, mods, dsk, sn, rn, wo, n2, w13, w2, ctx_len):
    b, n, d = xj.shape
    off = ctx_len // ROW_TILE
    nt = n // ROW_TILE - off
    lat = lambda w: pl.BlockSpec((1, ROW_TILE, w), lambda i, t: (i, t + off, 0))
    return pl.pallas_call(
        _post_ssm_kernel,
        grid=(b, nt),
        in_specs=[lat(d)] * 8 + [pl.BlockSpec((1, 1, 6, d), lambda i, t: (i, 1, 0, 0)),
                                 _const_spec(dsk.shape), _const_spec(sn.shape), _const_spec(rn.shape),
                                 _const_spec(wo.shape), _const_spec(n2.shape), _const_spec(w13.shape),
                                 _const_spec(w2.shape)],
        out_specs=pl.BlockSpec((1, ROW_TILE, d), lambda i, t: (i, t, 0)),
        out_shape=jax.ShapeDtypeStruct((b, nt * ROW_TILE, d), F32),
        compiler_params=_cparams(("parallel", "arbitrary")), name="post_ssm",
    )(xj, yf, yb, xs, z, rf, rb, rg, mods, dsk, sn, rn, wo, n2, w13, w2)


def _ssm_layer_params(w_in, conv_w, conv_b, dt_bias, a_log, d_skip, ssd_norm, ret_logit, ret_norm, w_out):
    z, xbc, dt, rq, rk, rv, rg = jnp.split(
        w_in, np.cumsum([SSD_DIN, XBC, 2 * SSD_NH, RET_H * RET_DK, RET_H * RET_DK, RET_H * RET_DV]).tolist(), axis=1)
    dt = jnp.pad(dt, ((0, 0), (0, LANES - 2 * SSD_NH)))
    win = jnp.concatenate([z, xbc, rq, rk, rv, rg, dt], axis=1).astype(BF16)
    pad_row = lambda a: jnp.pad(a.reshape(-1), (0, LANES - a.size)).reshape(1, LANES)
    return dict(win=win, cw=conv_w, cb=conv_b.reshape(1, -1), dtb=pad_row(dt_bias), alog=pad_row(a_log),
                dsk=jnp.repeat(d_skip, SSD_P).reshape(1, -1), sn=ssd_norm.reshape(1, -1), logit=pad_row(ret_logit),
                rn=ret_norm.reshape(1, -1), wo=w_out.astype(BF16))


def _ssm_layer(xj, mods, prm, tabs, n1, n2, w13, w2, ctx_len):
    z, xs, bc, dt, rq, rk, rv, rg = _pre_ssm(xj, mods, n1, prm['win'], prm['cw'], prm['cb'], prm['dtb'],
                                             tabs[0], tabs[1], ctx_len)
    yf, yb = _ssd_scan(xs, bc, dt, prm['alog'], ctx_len)
    rf, rb = _ret_scan(rq, rk, rv, prm['logit'], ctx_len)
    return _post_ssm(xj, yf, yb, xs, z, rf, rb, rg, mods, prm['dsk'], prm['sn'], prm['rn'], prm['wo'], n2, w13, w2,
                     ctx_len)


def _rope_angles(seq_len, ctx_len, dim):
    rows = seq_len // GRID_W
    rr, cc = jnp.meshgrid(jnp.arange(rows, dtype=F32), jnp.arange(GRID_W, dtype=F32), indexing='ij')
    quarter = dim // 4
    inv = ROPE_THETA ** (-jnp.arange(quarter, dtype=F32) / quarter)
    ang = jnp.concatenate([rr.reshape(-1)[:, None] * inv, cc.reshape(-1)[:, None] * inv], axis=-1)
    cos = jnp.concatenate([jnp.ones((ctx_len, dim // 2), F32), jnp.cos(ang)], axis=0)
    sin = jnp.concatenate([jnp.zeros((ctx_len, dim // 2), F32), jnp.sin(ang)], axis=0)
    return cos, sin


def _interleaved_tables(cos, sin):
    c = jnp.repeat(cos, 2, axis=-1)
    s = jnp.stack([-sin, sin], axis=-1).reshape(sin.shape[0], -1)
    return c, s


def _attn_layer_params(w_in, gqa_qn, gqa_kn, mla_qa_norm, mla_wq_b, mla_kva_norm, mla_wkv_b, mla_qn, mla_kn, w_out):
    d = w_in.shape[0]
    qa, ka, va, ql, kvl, kr = jnp.split(w_in, np.cumsum([512, 128, 128, MLA_Q_RANK, MLA_KV_RANK]).tolist(), axis=1)
    qa = qa.reshape(d, GQA_HQ, HEAD_DIM)
    qa = jnp.concatenate([jnp.concatenate([qa[:, j], qa[:, j + 4]], axis=1) for j in range(4)], axis=1)
    kr = jnp.pad(kr, ((0, 0), (MLA_DNOPE, LANES - MLA_DQK)))
    win = jnp.concatenate([qa, ka, va, ql, kvl, kr], axis=1).astype(BF16)
    wqb = jnp.pad(mla_wq_b.reshape(MLA_Q_RANK, MLA_H, MLA_DQK), ((0, 0), (0, 0), (0, LANES - MLA_DQK)))
    wqb = wqb.reshape(MLA_Q_RANK, MLA_H * LANES).astype(BF16)
    wkv = mla_wkv_b.reshape(MLA_KV_RANK, MLA_H, MLA_DNOPE + HEAD_DIM)
    wk = jnp.pad(wkv[:, :, :MLA_DNOPE], ((0, 0), (0, 0), (0, LANES - MLA_DNOPE))).reshape(MLA_KV_RANK, MLA_H * LANES)
    wv = wkv[:, :, MLA_DNOPE:].reshape(MLA_KV_RANK, MLA_H * HEAD_DIM)
    pad_gain = lambda g: jnp.pad(g, (0, LANES - MLA_DQK)).reshape(1, LANES)
    woa = w_out[:512].reshape(GQA_HQ, HEAD_DIM, d)
    woa = jnp.concatenate([jnp.concatenate([woa[j], woa[j + 4]], axis=0) for j in range(4)], axis=0)
    wo = jnp.concatenate([woa, w_out[512:]], axis=0).astype(BF16)
    return dict(win=win, gq=jnp.tile(gqa_qn, 2).reshape(1, LANES), gk=jnp.tile(gqa_kn, 2).reshape(1, LANES),
                gqa=mla_qa_norm.reshape(1, -1), gkva=mla_kva_norm.reshape(1, -1), wqb=wqb, wk=wk.astype(BF16),
                wv=wv.astype(BF16), gmq=pad_gain(mla_qn), gmk=pad_gain(mla_kn), wo=wo)


def _attn_tables(seq_len, ctx_len):
    cg, sg = _interleaved_tables(*_rope_angles(seq_len, ctx_len, HEAD_DIM))
    cg, sg = jnp.tile(cg, (1, 2)), jnp.tile(sg, (1, 2))
    cm, sm = _interleaved_tables(*_rope_angles(seq_len, ctx_len, MLA_DROPE))
    n = cm.shape[0]
    cm = jnp.concatenate([jnp.ones((n, MLA_DNOPE), F32), cm, jnp.ones((n, LANES - MLA_DQK), F32)], axis=1)
    sm = jnp.pad(sm, ((0, 0), (MLA_DNOPE, LANES - MLA_DQK)))
    return cg, sg, cm, sm


def _layer_mods(mods_i, batch):
    d = mods_i.shape[1] // 6
    lat = mods_i[:batch].reshape(batch, 1, 6, d)
    ctx = jnp.broadcast_to(mods_i[batch].reshape(1, 1, 6, d), (batch, 1, 6, d))
    return jnp.concatenate([ctx, lat], axis=1)


def _attention_layer(ctx, x, mods, prm, tabs, n1, n2, w13, w2):
    ctx_len = ctx.shape[1]
    qg, kg, vglo, vghi, qm, km, vmlo, vmhi = _pre_attn(
        ctx, x, mods, n1, prm['win'], prm['gq'], prm['gk'], prm['gqa'], prm['gkva'], prm['wqb'], prm['wk'], prm['wv'],
        prm['gmq'], prm['gmk'], *tabs)
    oa = _attention(qg, kg, vglo, vghi, shared_k=True, ctx_len=ctx_len)
    ob = _attention(qm, km, vmlo, vmhi, shared_k=False, ctx_len=ctx_len)
    return _post_attn(ctx, x, oa, ob, mods, prm['wo'], n2, w13, w2)


def _all_mods(c, c_ctx, mod_w, mod_b):
    batch = c.shape[0]
    rows = -(-(batch + 1) // 8) * 8
    cvec = jnp.concatenate([c, c_ctx[None, :], jnp.zeros((rows - batch - 1, c.shape[1]), F32)], axis=0)
    mods = _modulation(cvec, mod_w, mod_b)
    return [_layer_mods(mods[i], batch) for i in range(mod_w.shape[0])]


def kernel(x, c, ctx, c_ctx, mod_w, mod_b, norm1_w, norm2_w, ffn_w13, ffn_w2, attn_w_in, gqa_qn, gqa_kn, mla_qa_norm, mla_wq_b, mla_kva_norm, mla_wkv_b, mla_qn, mla_kn, attn_w_out, ssm_w_in, ssd_conv_w, ssd_conv_b, ssd_dt_bias, ssd_a_log, ssd_d, ssd_norm, ret_decay_logit, ret_norm, ssm_w_out):
    ctx_len, seq_len = ctx.shape[1], x.shape[1]
    assert ctx_len % ROW_TILE == 0 and seq_len % ROW_TILE == 0 and seq_len % GRID_W == 0
    assert mod_w.shape[0] == 2, "one attention layer followed by one SSM layer"
    mods = _all_mods(c, c_ctx, mod_w, mod_b)
    tabs = _attn_tables(seq_len, ctx_len)
    prm0 = _attn_layer_params(attn_w_in[0], gqa_qn[0], gqa_kn[0], mla_qa_norm[0], mla_wq_b[0], mla_kva_norm[0],
                              mla_wkv_b[0], mla_qn[0], mla_kn[0], attn_w_out[0])
    xj = _attention_layer(ctx, x, mods[0], prm0, tabs, norm1_w[0].reshape(1, -1), norm2_w[0].reshape(1, -1),
                          ffn_w13[0].astype(BF16), ffn_w2[0].astype(BF16))
    prm1 = _ssm_layer_params(ssm_w_in[0], ssd_conv_w[0], ssd_conv_b[0], ssd_dt_bias[0], ssd_a_log[0], ssd_d[0],
                             ssd_norm[0], ret_decay_logit[0], ret_norm[0], ssm_w_out[0])
    return _ssm_layer(xj, mods[1], prm1, tabs, norm1_w[1].reshape(1, -1), norm2_w[1].reshape(1, -1),
                      ffn_w13[1].astype(BF16), ffn_w2[1].astype(BF16), ctx_len)


def _debug_layer1(x, c, ctx, c_ctx, mod_w, mod_b, norm1_w, norm2_w, ffn_w13, ffn_w2, ssm_w_in, ssd_conv_w, ssd_conv_b,
                  ssd_dt_bias, ssd_a_log, ssd_d, ssd_norm, ret_decay_logit, ret_norm, ssm_w_out, **_):
    ctx_len, seq_len = ctx.shape[1], x.shape[1]
    mods = _all_mods(c, c_ctx, mod_w, mod_b)
    tabs = _attn_tables(seq_len, ctx_len)
    xj = jnp.concatenate([ctx, x], axis=1)
    prm1 = _ssm_layer_params(ssm_w_in[0], ssd_conv_w[0], ssd_conv_b[0], ssd_dt_bias[0], ssd_a_log[0], ssd_d[0],
                             ssd_norm[0], ret_decay_logit[0], ret_norm[0], ssm_w_out[0])
    return _ssm_layer(xj, mods[1], prm1, tabs, norm1_w[1].reshape(1, -1), norm2_w[1].reshape(1, -1),
                      ffn_w13[1].astype(BF16), ffn_w2[1].astype(BF16), ctx_len)


def _debug_layer0(x, c, ctx, c_ctx, mod_w, mod_b, norm1_w, norm2_w, ffn_w13, ffn_w2, attn_w_in, gqa_qn, gqa_kn,
                  mla_qa_norm, mla_wq_b, mla_kva_norm, mla_wkv_b, mla_qn, mla_kn, attn_w_out, **_):
    ctx_len, seq_len = ctx.shape[1], x.shape[1]
    mods = _all_mods(c, c_ctx, mod_w, mod_b)
    prm = _attn_layer_params(attn_w_in[0], gqa_qn[0], gqa_kn[0], mla_qa_norm[0], mla_wq_b[0], mla_kva_norm[0],
                             mla_wkv_b[0], mla_qn[0], mla_kn[0], attn_w_out[0])
    return _attention_layer(ctx, x, mods[0], prm, _attn_tables(seq_len, ctx_len), norm1_w[0].reshape(1, -1),
                            norm2_w[0].reshape(1, -1), ffn_w13[0].astype(BF16), ffn_w2[0].astype(BF16))
```
